```python
import jax, jax.numpy as jnp
from jax import lax
import numpy as np

D_MODEL = 2048
BATCH = 1
SEQ = 8192
DEPTH = 4

N_MIXERS = 2
N_HEADS = 16
HEAD_DIM = D_MODEL // N_HEADS
ROT_DIM = HEAD_DIM // 4
ROPE_THETA = 500000.0
EPS = 1e-6
A_KV_HEADS = 4
IDX_HEADS = 16
IDX_DIM = 64
IDX_ROT_DIM = IDX_DIM // 4
IDX_TOPK_MAX = 256
A_QBLOCK = 128
A_IN = N_HEADS * HEAD_DIM + 2 * A_KV_HEADS * HEAD_DIM + IDX_HEADS * IDX_DIM + IDX_DIM + IDX_HEADS
MOBA_BLOCK = 256
MOBA_TOPK = 3
B_QBLOCK = 32
B_IN = 3 * N_HEADS * HEAD_DIM
D_FF = -(-8 * D_MODEL // (3 * 256)) * 256
N_A = (DEPTH + 1) // 2
N_B = DEPTH // 2

kernel_name = "hybrid_dsa_moba_sandwich_adaln"


def rms_norm(x, g):
    xf = x.astype(jnp.float32)
    y = xf * lax.rsqrt(jnp.mean(xf * xf, axis=-1, keepdims=True) + EPS)
    return (y * g.astype(jnp.float32)).astype(x.dtype)


def partial_rope(x, positions, rot_dim):
    half = rot_dim // 2
    inv_freq = ROPE_THETA ** (-jnp.arange(half, dtype=jnp.float32) / half)
    ang = positions.astype(jnp.float32)[:, :, None] * inv_freq
    cos = jnp.cos(ang)[:, :, None, :]
    sin = jnp.sin(ang)[:, :, None, :]
    xf = x.astype(jnp.float32)
    x1 = xf[..., :half]
    x2 = xf[..., half:rot_dim]
    out = jnp.concatenate([x1 * cos - x2 * sin, x2 * cos + x1 * sin, xf[..., rot_dim:]], axis=-1)
    return out.astype(x.dtype)


def dsa_mixer(h, positions, w_in, w_o):
    B, S, _ = h.shape
    sizes = [N_HEADS * HEAD_DIM, A_KV_HEADS * HEAD_DIM, A_KV_HEADS * HEAD_DIM,
             IDX_HEADS * IDX_DIM, IDX_DIM]
    cuts = [int(v) for v in np.cumsum(sizes)]
    q, k, v, qi, ki, wi = jnp.split(h @ w_in, cuts, axis=-1)
    q = partial_rope(q.reshape(B, S, N_HEADS, HEAD_DIM), positions, ROT_DIM)
    k = partial_rope(k.reshape(B, S, A_KV_HEADS, HEAD_DIM), positions, ROT_DIM)
    v = v.reshape(B, S, A_KV_HEADS, HEAD_DIM)
    qi = partial_rope(qi.reshape(B, S, IDX_HEADS, IDX_DIM), positions, IDX_ROT_DIM)
    ki = partial_rope(ki.reshape(B, S, 1, IDX_DIM), positions, IDX_ROT_DIM)[:, :, 0]
    wi = wi.astype(jnp.float32) * (IDX_HEADS ** -0.5 * IDX_DIM ** -0.5)
    n_sel = min(IDX_TOPK_MAX, S // 4)
    rep = N_HEADS // A_KV_HEADS
    scale = HEAD_DIM ** -0.5
    key_pos = jnp.arange(S)

    def block(i):
        start = i * A_QBLOCK
        qpos = start + jnp.arange(A_QBLOCK)
        qb = lax.dynamic_slice_in_dim(q, start, A_QBLOCK, axis=1)
        qib = lax.dynamic_slice_in_dim(qi, start, A_QBLOCK, axis=1)
        wib = lax.dynamic_slice_in_dim(wi, start, A_QBLOCK, axis=1)
        dots = jnp.einsum('bqhd,bsd->bqhs', qib, ki, preferred_element_type=jnp.float32)
        isc = jnp.einsum('bqhs,bqh->bqs', jax.nn.relu(dots), wib)
        causal = key_pos[None, :] <= qpos[:, None]
        isc = jnp.where(causal[None], isc, -jnp.inf)
        _, sel = lax.top_k(isc, n_sel)
        valid = sel <= qpos[None, :, None]
        kg = jax.vmap(lambda kb, ib: kb[ib])(k, sel)
        vg = jax.vmap(lambda vb, ib: vb[ib])(v, sel)
        qg = qb.reshape(B, A_QBLOCK, A_KV_HEADS, rep, HEAD_DIM)
        logits = jnp.einsum('bqgrd,bqkgd->bqgrk', qg, kg, preferred_element_type=jnp.float32) * scale
        logits = jnp.where(valid[:, :, None, None, :], logits, -jnp.inf)
        p = jax.nn.softmax(logits, axis=-1)
        o = jnp.einsum('bqgrk,bqkgd->bqgrd', p.astype(vg.dtype), vg)
        return o.reshape(B, A_QBLOCK, N_HEADS * HEAD_DIM)

    out = lax.map(block, jnp.arange(S // A_QBLOCK))
    out = jnp.transpose(out, (1, 0, 2, 3)).reshape(B, S, N_HEADS * HEAD_DIM)
    return out @ w_o


def moba_mixer(h, positions, w_in, w_o):
    B, S, _ = h.shape
    q, k, v = jnp.split(h @ w_in, 3, axis=-1)
    q = partial_rope(q.reshape(B, S, N_HEADS, HEAD_DIM), positions, ROT_DIM)
    k = partial_rope(k.reshape(B, S, N_HEADS, HEAD_DIM), positions, ROT_DIM)
    v = v.reshape(B, S, N_HEADS, HEAD_DIM)
    nb = -(-S // MOBA_BLOCK)
    pad = nb * MOBA_BLOCK - S
    kp = jnp.pad(k, ((0, 0), (0, pad), (0, 0), (0, 0)))
    vp = jnp.pad(v, ((0, 0), (0, pad), (0, 0), (0, 0)))
    kblk = kp.reshape(B, nb, MOBA_BLOCK, N_HEADS, HEAD_DIM)
    vblk = vp.reshape(B, nb, MOBA_BLOCK, N_HEADS, HEAD_DIM)
    kmean = jnp.mean(kblk.astype(jnp.float32), axis=2)
    k_bh = jnp.transpose(kblk, (0, 3, 1, 2, 4))
    v_bh = jnp.transpose(vblk, (0, 3, 1, 2, 4))
    n_sel = min(MOBA_TOPK, nb)
    scale = HEAD_DIM ** -0.5
    blk_ids = jnp.arange(nb)
    bi = jnp.arange(B)[:, None, None, None]
    hi = jnp.arange(N_HEADS)[None, None, :, None]

    def block(i):
        start = i * B_QBLOCK
        qpos = start + jnp.arange(B_QBLOCK)
        j = start // MOBA_BLOCK
        qb = lax.dynamic_slice_in_dim(q, start, B_QBLOCK, axis=1)
        gate = jnp.einsum('bqhd,bnhd->bqhn', qb.astype(jnp.float32), kmean)
        gate = jnp.where((blk_ids < j)[None, None, None, :], gate, -jnp.inf)
        _, sel = lax.top_k(gate, n_sel)
        sel_valid = sel < j
        kg = k_bh[bi, hi, sel]
        vg = v_bh[bi, hi, sel]
        past = jnp.einsum('bqhd,bqhnkd->bqhnk', qb, kg, preferred_element_type=jnp.float32) * scale
        past = jnp.where(sel_valid[..., None], past, -jnp.inf).reshape(B, B_QBLOCK, N_HEADS, n_sel * MOBA_BLOCK)
        own_k = lax.dynamic_slice_in_dim(kp, j * MOBA_BLOCK, MOBA_BLOCK, axis=1)
        own_v = lax.dynamic_slice_in_dim(vp, j * MOBA_BLOCK, MOBA_BLOCK, axis=1)
        own = jnp.einsum('bqhd,bkhd->bqhk', qb, own_k, preferred_element_type=jnp.float32) * scale
        own_pos = j * MOBA_BLOCK + jnp.arange(MOBA_BLOCK)
        own = jnp.where((own_pos[None, :] <= qpos[:, None])[None, :, None, :], own, -jnp.inf)
        p = jax.nn.softmax(jnp.concatenate([past, own], axis=-1), axis=-1)
        p_past = p[..., :n_sel * MOBA_BLOCK].reshape(B, B_QBLOCK, N_HEADS, n_sel, MOBA_BLOCK).astype(vg.dtype)
        p_own = p[..., n_sel * MOBA_BLOCK:].astype(own_v.dtype)
        o = (jnp.einsum('bqhnk,bqhnkd->bqhd', p_past, vg)
             + jnp.einsum('bqhk,bkhd->bqhd', p_own, own_v))
        return o.reshape(B, B_QBLOCK, N_HEADS * HEAD_DIM)

    out = lax.map(block, jnp.arange(S // B_QBLOCK))
    out = jnp.transpose(out, (1, 0, 2, 3)).reshape(B, S, N_HEADS * HEAD_DIM)
    return out @ w_o


def swiglu(h, w_in, w_out):
    u, g = jnp.split(h @ w_in, 2, axis=-1)
    return (jax.nn.silu(g) * u) @ w_out


def setup_inputs(seed: int = 0) -> dict:
    key = jax.random.key(seed)
    ks = jax.random.split(key, 13)
    f32 = jnp.float32
    hd = N_HEADS * HEAD_DIM
    x = jax.random.normal(ks[0], (BATCH, SEQ, D_MODEL), f32)
    c = jax.random.normal(ks[1], (BATCH, D_MODEL), f32)
    start = jax.random.randint(ks[2], (BATCH, 1), 0, 4096, dtype=jnp.int32)
    positions = start + jnp.arange(SEQ, dtype=jnp.int32)[None, :]
    a_w_in = jax.random.normal(ks[3], (N_A, D_MODEL, A_IN), f32) * D_MODEL ** -0.5
    a_w_o = jax.random.normal(ks[4], (N_A, hd, D_MODEL), f32) * hd ** -0.5
    b_w_in = jax.random.normal(ks[5], (N_B, D_MODEL, B_IN), f32) * D_MODEL ** -0.5
    b_w_o = jax.random.normal(ks[6], (N_B, hd, D_MODEL), f32) * hd ** -0.5
    ada_w = jax.random.normal(ks[7], (DEPTH, D_MODEL, 6 * D_MODEL), f32) * (0.5 * D_MODEL ** -0.5)
    ada_b = jax.random.normal(ks[8], (DEPTH, 6 * D_MODEL), f32) * 0.01
    norm_g = 1.0 + 0.02 * jax.random.normal(ks[9], (DEPTH, 4, D_MODEL), f32)
    ffn_w_in = jax.random.normal(ks[10], (DEPTH, D_MODEL, 2 * D_FF), f32) * D_MODEL ** -0.5
    ffn_w_out = jax.random.normal(ks[11], (DEPTH, D_FF, D_MODEL), f32) * D_FF ** -0.5
    return {"x": x, "c": c, "positions": positions,
            "a_w_in": a_w_in, "a_w_o": a_w_o, "b_w_in": b_w_in, "b_w_o": b_w_o,
            "ada_w": ada_w, "ada_b": ada_b, "norm_g": norm_g,
            "ffn_w_in": ffn_w_in, "ffn_w_out": ffn_w_out}


def reference(x, c, positions, a_w_in, a_w_o, b_w_in, b_w_o, ada_w, ada_b, norm_g, ffn_w_in, ffn_w_out):
    c_act = jax.nn.silu(c)
    for i in range(DEPTH):
        mod = (c_act @ ada_w[i] + ada_b[i])[:, None, :]
        sh1, sc1, g1, sh2, sc2, g2 = jnp.split(mod, 6, axis=-1)
        h = rms_norm(x, norm_g[i, 0]) * (1 + sc1) + sh1
        if i % N_MIXERS == 0:
            y = dsa_mixer(h, positions, a_w_in[i // 2], a_w_o[i // 2])
        else:
            y = moba_mixer(h, positions, b_w_in[i // 2], b_w_o[i // 2])
        x = x + g1 * rms_norm(y, norm_g[i, 1])
        h = rms_norm(x, norm_g[i, 2]) * (1 + sc2) + sh2
        y = swiglu(h, ffn_w_in[i], ffn_w_out[i])
        x = x + g2 * rms_norm(y, norm_g[i, 3])
    return x
```

```python
import functools
import math

import numpy as np
import jax
import jax.numpy as jnp
from jax import lax
from jax.experimental import pallas as pl
from jax.experimental.pallas import tpu as pltpu

N_HEADS = 16
HEAD_DIM = 128
ROT_DIM = HEAD_DIM // 4
ROPE_THETA = 500000.0
EPS = 1e-6
A_KV_HEADS = 4
IDX_HEADS = 16
IDX_DIM = 64
IDX_ROT_DIM = IDX_DIM // 4
IDX_TOPK_MAX = 256
MOBA_BLOCK = 256
MOBA_TOPK = 3

LANES = 128
VMEM_LIMIT_BYTES = 56 * 1024 * 1024

MXU_DTYPE = jnp.bfloat16
NEG = -1e30
LOG2E = math.log2(math.e)

_NT = (((1,), (1,)), ((), ()))


def _cparams(sem):
    return pltpu.CompilerParams(dimension_semantics=sem, vmem_limit_bytes=VMEM_LIMIT_BYTES)


def _ada_kernel(ct_ref, w_ref, b_ref, o_ref):
    ct = ct_ref[...]
    ca = ct * (1.0 / (1.0 + jnp.exp(-ct)))
    o_ref[...] = jnp.sum(ca * w_ref[...], axis=0, keepdims=True) + b_ref[...]


def ada_modulation(c, ada_w, ada_b, tn=1024):
    depth, d, n = ada_w.shape
    ct = c.reshape(d, 1)
    return pl.pallas_call(
        _ada_kernel,
        grid=(depth, n // tn),
        in_specs=[
            pl.BlockSpec((d, 1), lambda l, j: (0, 0)),
            pl.BlockSpec((None, d, tn), lambda l, j: (l, 0, j)),
            pl.BlockSpec((None, 1, tn), lambda l, j: (l, 0, j)),
        ],
        out_specs=pl.BlockSpec((None, 1, tn), lambda l, j: (l, 0, j)),
        out_shape=jax.ShapeDtypeStruct((depth, 1, n), jnp.float32),
        compiler_params=_cparams(("parallel", "parallel")),
        name="ada_modulation",
    )(ct, ada_w, ada_b.reshape(depth, 1, n))


def _norm_mod(x, g, sc, sh):
    ms = jnp.mean(x * x, axis=-1, keepdims=True)
    y = x * lax.rsqrt(ms + EPS)
    return (y * g) * (1.0 + sc) + sh


def _nm_matmul_kernel(x_ref, g_ref, sc_ref, sh_ref, w_ref, o_ref, h_ref):
    @pl.when(pl.program_id(1) == 0)
    def _():
        h_ref[...] = _norm_mod(x_ref[...], g_ref[...], sc_ref[...], sh_ref[...]).astype(h_ref.dtype)

    o_ref[...] = jnp.dot(h_ref[...], w_ref[...].astype(MXU_DTYPE),
                         preferred_element_type=jnp.float32).astype(o_ref.dtype)


def norm_mod_matmul(x, g, sc, sh, w, layer, n_cols, out_dtype, tm=1024, tn=512):
    s, d = x.shape
    return pl.pallas_call(
        _nm_matmul_kernel,
        grid=(s // tm, n_cols // tn),
        in_specs=[
            pl.BlockSpec((tm, d), lambda i, j: (i, 0)),
            pl.BlockSpec((1, d), lambda i, j: (0, 0)),
            pl.BlockSpec((1, d), lambda i, j: (0, 0)),
            pl.BlockSpec((1, d), lambda i, j: (0, 0)),
            pl.BlockSpec((None, d, tn), lambda i, j: (layer, 0, j)),
        ],
        out_specs=pl.BlockSpec((tm, tn), lambda i, j: (i, j)),
        out_shape=jax.ShapeDtypeStruct((s, n_cols), out_dtype),
        scratch_shapes=[pltpu.VMEM((tm, d), MXU_DTYPE)],
        compiler_params=_cparams(("parallel", "arbitrary")),
        name="norm_mod_matmul",
    )(x, g, sc, sh, w)


def _nm_swiglu_kernel(x_ref, g_ref, sc_ref, sh_ref, wu_ref, wg_ref, o_ref, h_ref):
    @pl.when(pl.program_id(1) == 0)
    def _():
        h_ref[...] = _norm_mod(x_ref[...], g_ref[...], sc_ref[...], sh_ref[...]).astype(h_ref.dtype)

    h = h_ref[...]
    u = jnp.dot(h, wu_ref[...].astype(MXU_DTYPE), preferred_element_type=jnp.float32)
    gt = jnp.dot(h, wg_ref[...].astype(MXU_DTYPE), preferred_element_type=jnp.float32)
    o_ref[...] = ((gt * (1.0 / (1.0 + jnp.exp(-gt)))) * u).astype(o_ref.dtype)


def norm_mod_swiglu(x, g, sc, sh, w_in, layer, tm=1024, tn=512):
    s, d = x.shape
    f = w_in.shape[2] // 2
    nj = f // tn
    return pl.pallas_call(
        _nm_swiglu_kernel,
        grid=(s // tm, nj),
        in_specs=[
            pl.BlockSpec((tm, d), lambda i, j: (i, 0)),
            pl.BlockSpec((1, d), lambda i, j: (0, 0)),
            pl.BlockSpec((1, d), lambda i, j: (0, 0)),
            pl.BlockSpec((1, d), lambda i, j: (0, 0)),
            pl.BlockSpec((None, d, tn), lambda i, j: (layer, 0, j)),
            pl.BlockSpec((None, d, tn), lambda i, j: (layer, 0, j + nj)),
        ],
        out_specs=pl.BlockSpec((tm, tn), lambda i, j: (i, j)),
        out_shape=jax.ShapeDtypeStruct((s, f), MXU_DTYPE),
        scratch_shapes=[pltpu.VMEM((tm, d), MXU_DTYPE)],
        compiler_params=_cparams(("parallel", "arbitrary")),
        name="norm_mod_swiglu",
    )(x, g, sc, sh, w_in, w_in)


def _mm_postnorm_kernel(a_ref, w_ref, x_ref, g_ref, gate_ref, o_ref, acc_ref):
    k = pl.program_id(1)

    @pl.when(k == 0)
    def _():
        acc_ref[...] = jnp.zeros_like(acc_ref)

    acc_ref[...] += jnp.dot(a_ref[...], w_ref[...].astype(MXU_DTYPE), preferred_element_type=jnp.float32)

    @pl.when(k == pl.num_programs(1) - 1)
    def _():
        y = acc_ref[...]
        ms = jnp.mean(y * y, axis=-1, keepdims=True)
        yn = (y * lax.rsqrt(ms + EPS)) * g_ref[...]
        o_ref[...] = x_ref[...] + gate_ref[...] * yn


def matmul_postnorm_residual(a, w, layer, x, g, gate, tm=512, tk=512):
    s, kdim = a.shape
    d = w.shape[2]
    return pl.pallas_call(
        _mm_postnorm_kernel,
        grid=(s // tm, kdim // tk),
        in_specs=[
            pl.BlockSpec((tm, tk), lambda i, k: (i, k)),
            pl.BlockSpec((None, tk, d), lambda i, k: (layer, k, 0)),
            pl.BlockSpec((tm, d), lambda i, k: (i, 0)),
            pl.BlockSpec((1, d), lambda i, k: (0, 0)),
            pl.BlockSpec((1, d), lambda i, k: (0, 0)),
        ],
        out_specs=pl.BlockSpec((tm, d), lambda i, k: (i, 0)),
        out_shape=jax.ShapeDtypeStruct((s, d), jnp.float32),
        scratch_shapes=[pltpu.VMEM((tm, d), jnp.float32)],
        compiler_params=_cparams(("parallel", "arbitrary")),
        name="matmul_postnorm_residual",
    )(a, w, x, g, gate)


def rope_tables(positions, rot_dim, period):
    half = rot_dim // 2
    inv_freq = ROPE_THETA ** (-jnp.arange(half, dtype=jnp.float32) / half)
    ang = positions.astype(jnp.float32)[:, None] * inv_freq
    cos, sin = jnp.cos(ang), jnp.sin(ang)
    s = positions.shape[0]
    one = jnp.ones((s, period - rot_dim), jnp.float32)
    zero = jnp.zeros((s, period - rot_dim), jnp.float32)
    zh = jnp.zeros((s, half), jnp.float32)
    reps = LANES // period
    c = jnp.tile(jnp.concatenate([cos, cos, one], axis=1), (1, reps))
    sa = jnp.tile(jnp.concatenate([-sin, zh, zero], axis=1), (1, reps))
    sb = jnp.tile(jnp.concatenate([zh, sin, zero], axis=1), (1, reps))
    return c, sa, sb


def _rope_lanes(x, c, sa, sb, half):
    return x * c + pltpu.roll(x, LANES - half, 1) * sa + pltpu.roll(x, half, 1) * sb


def _rope_kernel(y_ref, c_ref, sa_ref, sb_ref, o_ref, *, half, out_scale, n_groups):
    c, sa, sb = c_ref[...], sa_ref[...], sb_ref[...]
    for h in range(n_groups):
        sl = slice(h * LANES, (h + 1) * LANES)
        r = _rope_lanes(y_ref[:, sl], c, sa, sb, half)
        o_ref[:, sl] = (r * out_scale).astype(o_ref.dtype)


def _rope_kmean_kernel(y_ref, c_ref, sa_ref, sb_ref, o_ref, km_ref, *, half, n_groups):
    c, sa, sb = c_ref[...], sa_ref[...], sb_ref[...]
    for h in range(n_groups):
        sl = slice(h * LANES, (h + 1) * LANES)
        r = _rope_lanes(y_ref[:, sl], c, sa, sb, half)
        o_ref[:, sl] = r.astype(o_ref.dtype)
        km_ref[:, sl] = jnp.mean(r, axis=0, keepdims=True)


def rope_cast(y, tables, col_start, n_cols, half, out_scale=1.0, tm=256):
    s = y.shape[0]
    cb = col_start // n_cols
    assert cb * n_cols == col_start
    tab = pl.BlockSpec((tm, LANES), lambda i: (i, 0))
    return pl.pallas_call(
        functools.partial(_rope_kernel, half=half, out_scale=out_scale, n_groups=n_cols // LANES),
        grid=(s // tm,),
        in_specs=[pl.BlockSpec((tm, n_cols), lambda i: (i, cb)), tab, tab, tab],
        out_specs=pl.BlockSpec((tm, n_cols), lambda i: (i, 0)),
        out_shape=jax.ShapeDtypeStruct((s, n_cols), MXU_DTYPE),
        compiler_params=_cparams(("parallel",)),
        name="rope_cast",
    )(y, *tables)


def _cast_kernel(y_ref, o_ref):
    o_ref[...] = y_ref[...].astype(o_ref.dtype)


def cast_cols(y, col_start, n_cols, tm=256):
    s = y.shape[0]
    cb = col_start // n_cols
    assert cb * n_cols == col_start
    return pl.pallas_call(
        _cast_kernel,
        grid=(s // tm,),
        in_specs=[pl.BlockSpec((tm, n_cols), lambda i: (i, cb))],
        out_specs=pl.BlockSpec((tm, n_cols), lambda i: (i, 0)),
        out_shape=jax.ShapeDtypeStruct((s, n_cols), MXU_DTYPE),
        compiler_params=_cparams(("parallel",)),
        name="cast_cols",
    )(y)


def rope_cast_kmean(y, tables, col_start, n_cols, half):
    s = y.shape[0]
    tm = MOBA_BLOCK
    cb = col_start // n_cols
    assert cb * n_cols == col_start
    tab = pl.BlockSpec((tm, LANES), lambda i: (i, 0))
    k, km = pl.pallas_call(
        functools.partial(_rope_kmean_kernel, half=half, n_groups=n_cols // LANES),
        grid=(s // tm,),
        in_specs=[pl.BlockSpec((tm, n_cols), lambda i: (i, cb)), tab, tab, tab],
        out_specs=[pl.BlockSpec((tm, n_cols), lambda i: (i, 0)),
                   pl.BlockSpec((None, 1, n_cols), lambda i: (i, 0, 0))],
        out_shape=[jax.ShapeDtypeStruct((s, n_cols), MXU_DTYPE),
                   jax.ShapeDtypeStruct((s // tm, 1, n_cols), jnp.float32)],
        compiler_params=_cparams(("parallel",)),
        name="rope_cast_kmean",
    )(y, *tables)
    return k, km.reshape(s // tm, n_cols)


def _idx_tail_kernel(y_ref, c_ref, sa_ref, sb_ref, kid_ref, wi_ref, *, half, w_scale):
    y = y_ref[...]
    r = _rope_lanes(y, c_ref[...], sa_ref[...], sb_ref[...], half)
    lane = lax.broadcasted_iota(jnp.int32, y.shape, 1)
    kid_ref[...] = jnp.where(lane < IDX_DIM, r, pltpu.roll(r, IDX_DIM, 1)).astype(kid_ref.dtype)
    wi_ref[...] = y * w_scale


def idx_tail(y, tables, tm=512):
    s = y.shape[0]
    blk = pl.BlockSpec((tm, LANES), lambda i: (i, 0))
    return pl.pallas_call(
        functools.partial(_idx_tail_kernel, half=IDX_ROT_DIM // 2,
                          w_scale=IDX_HEADS ** -0.5 * IDX_DIM ** -0.5),
        grid=(s // tm,),
        in_specs=[blk, blk, blk, blk],
        out_specs=[blk, blk],
        out_shape=[jax.ShapeDtypeStruct((s, LANES), MXU_DTYPE),
                   jax.ShapeDtypeStruct((s, LANES), jnp.float32)],
        compiler_params=_cparams(("parallel",)),
        name="idx_tail",
    )(y, *tables)


def _sortable_key(x):
    b = pltpu.bitcast(x, jnp.int32)
    return jnp.where(b < 0, b ^ jnp.int32(0x7FFFFFFF), b)


def _indexer_kernel(qi_ref, kid_ref, wi_ref, bias_ref, key_ref, *, tq, tk, n_sel):
    i = pl.program_id(0)
    n_tiles = bias_ref.shape[0]
    n_live = (i * tq + tq + tk - 1) // tk
    row = i * tq + lax.broadcasted_iota(jnp.int32, (tq, tk), 0)
    col0 = lax.broadcasted_iota(jnp.int32, (tq, tk), 1)
    lane = lax.broadcasted_iota(jnp.int32, (tq, LANES), 1)
    w = wi_ref[...]

    def score_tile(c, carry):
        kc = kid_ref[pl.ds(pl.multiple_of(c * tk, tk), tk), :]
        acc = jnp.zeros((tq, tk), jnp.float32)
        for p in range(IDX_HEADS // 2):
            qp = qi_ref[:, p * LANES:(p + 1) * LANES]
            zero = jnp.zeros_like(qp)
            d_lo = lax.dot_general(jnp.where(lane < IDX_DIM, qp, zero), kc, _NT,
                                   preferred_element_type=jnp.float32)
            d_hi = lax.dot_general(jnp.where(lane >= IDX_DIM, qp, zero), kc, _NT,
                                   preferred_element_type=jnp.float32)
            w_lo = w[:, IDX_DIM + 2 * p:IDX_DIM + 2 * p + 1]
            w_hi = w[:, IDX_DIM + 2 * p + 1:IDX_DIM + 2 * p + 2]
            acc = acc + jnp.maximum(d_lo, 0.0) * w_lo + jnp.maximum(d_hi, 0.0) * w_hi
        causal = (c * tk + col0) <= row
        key_ref[c] = _sortable_key(jnp.where(causal, acc, -jnp.inf))
        return carry

    lax.fori_loop(0, n_live, score_tile, 0)

    def count_ge(cand):
        def body(c, cnt):
            ind = jnp.where(key_ref[c] >= cand, 1, 0)
            for t in range(tk // LANES):
                cnt = cnt + ind[:, t * LANES:(t + 1) * LANES]
            return cnt
        cnt = lax.fori_loop(0, n_live, body, jnp.zeros((tq, LANES), jnp.int32))
        return jnp.sum(cnt, axis=-1, keepdims=True)

    int_min = jnp.int32(-2 ** 31)
    tau0 = jnp.full((tq, 1), int_min, jnp.int32)
    zero_c = jnp.zeros((tq, 1), jnp.int32)
    tau = jnp.where(count_ge(zero_c) >= n_sel, zero_c, tau0)

    def bit_step(b, tau):
        cand = tau + lax.shift_left(jnp.int32(1), 30 - b)
        return jnp.where(count_ge(cand) >= n_sel, cand, tau)

    tau = lax.fori_loop(0, 31, bit_step, tau)

    def write_live(c, carry):
        causal = (c * tk + col0) <= row
        sel = jnp.where(key_ref[c] >= tau, 0.0, NEG)
        bias_ref[c] = jnp.where(causal, sel, NEG).astype(bias_ref.dtype)
        return carry

    lax.fori_loop(0, n_live, write_live, 0)

    def write_dead(c, carry):
        bias_ref[c] = jnp.full((tq, tk), NEG, bias_ref.dtype)
        return carry

    lax.fori_loop(n_live, n_tiles, write_dead, 0)


def dsa_index_bias(qi, kid, wi, n_sel, tq=256, tk=512):
    s = qi.shape[0]
    return pl.pallas_call(
        functools.partial(_indexer_kernel, tq=tq, tk=tk, n_sel=n_sel),
        grid=(s // tq,),
        in_specs=[
            pl.BlockSpec((tq, IDX_HEADS * IDX_DIM), lambda i: (i, 0)),
            pl.BlockSpec((s, LANES), lambda i: (0, 0)),
            pl.BlockSpec((tq, LANES), lambda i: (i, 0)),
        ],
        out_specs=pl.BlockSpec((None, s // tk, tq, tk), lambda i: (i, 0, 0, 0)),
        out_shape=jax.ShapeDtypeStruct((s // tq, s // tk, tq, tk), MXU_DTYPE),
        scratch_shapes=[pltpu.VMEM((s // tk, tq, tk), jnp.int32)],
        compiler_params=_cparams(("parallel",)),
        name="dsa_index_bias",
    )(qi, kid, wi)


def _flash_update(h, s, v, m_ref, l_ref, acc_ref):
    m_prev = m_ref[h]
    m_new = jnp.maximum(m_prev, jnp.max(s, axis=-1, keepdims=True))
    alpha = jnp.exp2(m_prev - m_new)
    p = jnp.exp2(s - m_new)
    l_ref[h] = alpha * l_ref[h] + jnp.sum(p, axis=-1, keepdims=True)
    acc_ref[h] = alpha * acc_ref[h] + jnp.dot(p.astype(MXU_DTYPE), v, preferred_element_type=jnp.float32)
    m_ref[h] = m_new


def _flash_init(m_ref, l_ref, acc_ref):
    m_ref[...] = jnp.full(m_ref.shape, NEG, jnp.float32)
    l_ref[...] = jnp.zeros_like(l_ref)
    acc_ref[...] = jnp.zeros_like(acc_ref)


def _flash_finish(o_ref, l_ref, acc_ref):
    for h in range(N_HEADS):
        o_ref[:, h * HEAD_DIM:(h + 1) * HEAD_DIM] = (acc_ref[h] / l_ref[h]).astype(o_ref.dtype)


def _dsa_attn_kernel(q_ref, k_ref, v_ref, bias_ref, o_ref, m_ref, l_ref, acc_ref, *, tq, tk):
    i, j = pl.program_id(0), pl.program_id(1)

    @pl.when(j == 0)
    def _():
        _flash_init(m_ref, l_ref, acc_ref)

    @pl.when(j * tk < (i + 1) * tq)
    def _():
        bias = bias_ref[...].astype(jnp.float32)
        rep = N_HEADS // A_KV_HEADS
        for g in range(A_KV_HEADS):
            kg = k_ref[:, g * HEAD_DIM:(g + 1) * HEAD_DIM]
            vg = v_ref[:, g * HEAD_DIM:(g + 1) * HEAD_DIM]
            for r in range(rep):
                h = g * rep + r
                qh = q_ref[:, h * HEAD_DIM:(h + 1) * HEAD_DIM]
                s = lax.dot_general(qh, kg, _NT, preferred_element_type=jnp.float32) + bias
                _flash_update(h, s, vg, m_ref, l_ref, acc_ref)

    @pl.when(j == pl.num_programs(1) - 1)
    def _():
        _flash_finish(o_ref, l_ref, acc_ref)


def dsa_attention(q, k, v, bias, tq=256, tk=512):
    s = q.shape[0]
    kvw = A_KV_HEADS * HEAD_DIM

    def last_j(i):
        return ((i + 1) * tq - 1) // tk

    return pl.pallas_call(
        functools.partial(_dsa_attn_kernel, tq=tq, tk=tk),
        grid=(s // tq, s // tk),
        in_specs=[
            pl.BlockSpec((tq, N_HEADS * HEAD_DIM), lambda i, j: (i, 0)),
            pl.BlockSpec((tk, kvw), lambda i, j: (jnp.minimum(j, last_j(i)), 0)),
            pl.BlockSpec((tk, kvw), lambda i, j: (jnp.minimum(j, last_j(i)), 0)),
            pl.BlockSpec((None, None, tq, tk), lambda i, j: (i, jnp.minimum(j, last_j(i)), 0, 0)),
        ],
        out_specs=pl.BlockSpec((tq, N_HEADS * HEAD_DIM), lambda i, j: (i, 0)),
        out_shape=jax.ShapeDtypeStruct((s, N_HEADS * HEAD_DIM), MXU_DTYPE),
        scratch_shapes=[pltpu.VMEM((N_HEADS, tq, 1), jnp.float32),
                        pltpu.VMEM((N_HEADS, tq, 1), jnp.float32),
                        pltpu.VMEM((N_HEADS, tq, HEAD_DIM), jnp.float32)],
        compiler_params=_cparams(("parallel", "arbitrary")),
        name="dsa_attention",
    )(q, k, v, bias)


def _moba_attn_kernel(q_ref, k_ref, v_ref, km_ref, o_ref, m_ref, l_ref, acc_ref, sel_ref, *, n_sel):
    i, j = pl.program_id(0), pl.program_id(1)
    tq = q_ref.shape[0]
    lane = lax.broadcasted_iota(jnp.int32, (tq, LANES), 1)
    lane_f = lane.astype(jnp.float32)

    @pl.when(j == 0)
    def _():
        _flash_init(m_ref, l_ref, acc_ref)
        for h in range(N_HEADS):
            sl = slice(h * HEAD_DIM, (h + 1) * HEAD_DIM)
            gate = lax.dot_general(q_ref[:, sl], km_ref[:, sl].astype(MXU_DTYPE), _NT,
                                   preferred_element_type=jnp.float32)
            gate = jnp.where(lane < i, gate, -jnp.inf)
            bias = jnp.full((tq, LANES), NEG, jnp.float32)
            for _ in range(n_sel):
                top = jnp.max(gate, axis=-1, keepdims=True)
                first = jnp.min(jnp.where(gate == top, lane_f, float(LANES)), axis=-1, keepdims=True)
                hit = lane_f == first
                bias = jnp.where(hit, 0.0, bias)
                gate = jnp.where(hit, -jnp.inf, gate)
            sel_ref[h] = jnp.where(lane < i, bias, NEG)

    @pl.when(j <= i)
    def _():
        r = lax.broadcasted_iota(jnp.int32, (tq, tq), 0)
        c = lax.broadcasted_iota(jnp.int32, (tq, tq), 1)
        own = jnp.where(j == i, 1.0, 0.0)
        diag_bias = jnp.where(c <= r, 0.0, NEG) * own
        for h in range(N_HEADS):
            sl = slice(h * HEAD_DIM, (h + 1) * HEAD_DIM)
            blk_bias = jnp.sum(jnp.where(lane == j, sel_ref[h], 0.0), axis=-1, keepdims=True) * (1.0 - own)
            s = lax.dot_general(q_ref[:, sl], k_ref[:, sl], _NT, preferred_element_type=jnp.float32)
            s = s + (diag_bias + blk_bias)
            _flash_update(h, s, v_ref[:, sl], m_ref, l_ref, acc_ref)

    @pl.when(j == pl.num_programs(1) - 1)
    def _():
        _flash_finish(o_ref, l_ref, acc_ref)


def moba_attention(q, k, v, kmean_pad, n_sel):
    s, hd = q.shape
    t = MOBA_BLOCK
    nb = s // t
    return pl.pallas_call(
        functools.partial(_moba_attn_kernel, n_sel=n_sel),
        grid=(nb, nb),
        in_specs=[
            pl.BlockSpec((t, hd), lambda i, j: (i, 0)),
            pl.BlockSpec((t, hd), lambda i, j: (jnp.minimum(j, i), 0)),
            pl.BlockSpec((t, hd), lambda i, j: (jnp.minimum(j, i), 0)),
            pl.BlockSpec((LANES, hd), lambda i, j: (0, 0)),
        ],
        out_specs=pl.BlockSpec((t, hd), lambda i, j: (i, 0)),
        out_shape=jax.ShapeDtypeStruct((s, hd), MXU_DTYPE),
        scratch_shapes=[pltpu.VMEM((N_HEADS, t, 1), jnp.float32),
                        pltpu.VMEM((N_HEADS, t, 1), jnp.float32),
                        pltpu.VMEM((N_HEADS, t, HEAD_DIM), jnp.float32),
                        pltpu.VMEM((N_HEADS, t, LANES), jnp.float32)],
        compiler_params=_cparams(("parallel", "arbitrary")),
        name="moba_attention",
    )(q, k, v, kmean_pad)


def _dsa_mixer(x, g, sc, sh, w_in, layer, tabs_main, tabs_idx):
    s = x.shape[0]
    hd = N_HEADS * HEAD_DIM
    kvw = A_KV_HEADS * HEAD_DIM
    n_main = hd + 2 * kvw + IDX_HEADS * IDX_DIM
    n_tail = w_in.shape[2] - n_main
    y = norm_mod_matmul(x, g, sc, sh, w_in, layer, n_main, jnp.float32)
    w_tail = jnp.pad(w_in[layer, :, n_main:], ((0, 0), (0, LANES - n_tail)))[None]
    y_tail = norm_mod_matmul(x, g, sc, sh, w_tail, 0, LANES, jnp.float32, tn=LANES)
    q = rope_cast(y, tabs_main, 0, hd, ROT_DIM // 2, out_scale=HEAD_DIM ** -0.5 * LOG2E)
    k = rope_cast(y, tabs_main, hd, kvw, ROT_DIM // 2)
    v = cast_cols(y, hd + kvw, kvw)
    qi = rope_cast(y, tabs_idx, hd + 2 * kvw, IDX_HEADS * IDX_DIM, IDX_ROT_DIM // 2)
    kid, wi = idx_tail(y_tail, tabs_idx)
    n_sel = min(IDX_TOPK_MAX, s // 4)
    bias = dsa_index_bias(qi, kid, wi, n_sel)
    return dsa_attention(q, k, v, bias)


def _moba_mixer(x, g, sc, sh, w_in, layer, tabs_main):
    s = x.shape[0]
    hd = N_HEADS * HEAD_DIM
    y = norm_mod_matmul(x, g, sc, sh, w_in, layer, 3 * hd, jnp.float32)
    q = rope_cast(y, tabs_main, 0, hd, ROT_DIM // 2, out_scale=HEAD_DIM ** -0.5 * LOG2E)
    k, kmean = rope_cast_kmean(y, tabs_main, hd, hd, ROT_DIM // 2)
    v = cast_cols(y, 2 * hd, hd)
    nb = s // MOBA_BLOCK
    kmean_pad = jnp.pad(kmean, ((0, LANES - nb), (0, 0)))
    return moba_attention(q, k, v, kmean_pad, min(MOBA_TOPK, nb))


def kernel(x, c, positions, a_w_in, a_w_o, b_w_in, b_w_o, ada_w, ada_b, norm_g, ffn_w_in, ffn_w_out):
    b, s, d = x.shape
    assert b == 1 and s % MOBA_BLOCK == 0 and s // MOBA_BLOCK <= LANES
    depth = ada_w.shape[0]
    xs = x[0]
    mod = ada_modulation(c, ada_w, ada_b)
    pos = positions[0]
    tabs_main = rope_tables(pos, ROT_DIM, HEAD_DIM)
    tabs_idx = rope_tables(pos, IDX_ROT_DIM, IDX_DIM)
    for i in range(depth):
        sh1, sc1, g1, sh2, sc2, g2 = [mod[i, :, t * d:(t + 1) * d] for t in range(6)]
        ng = [norm_g[i, t][None, :] for t in range(4)]
        if i % 2 == 0:
            o = _dsa_mixer(xs, ng[0], sc1, sh1, a_w_in, i // 2, tabs_main, tabs_idx)
            w_o = a_w_o
        else:
            o = _moba_mixer(xs, ng[0], sc1, sh1, b_w_in, i // 2, tabs_main)
            w_o = b_w_o
        xs = matmul_postnorm_residual(o, w_o, i // 2, xs, ng[1], g1)
        act = norm_mod_swiglu(xs, ng[2], sc2, sh2, ffn_w_in, i)
        xs = matmul_postnorm_residual(act, ffn_w_out, i, xs, ng[3], g2)
    return xs[None]
```

```python
import functools
import math

import jax
import jax.numpy as jnp
from jax import lax
from jax.experimental import pallas as pl
from jax.experimental.pallas import tpu as pltpu

N_HEADS = 16
HEAD_DIM = 128
ROT_DIM = HEAD_DIM // 4
ROPE_THETA = 500000.0
EPS = 1e-6
A_KV_HEADS = 4
IDX_HEADS = 16
IDX_DIM = 64
IDX_ROT_DIM = IDX_DIM // 4
IDX_TOPK_MAX = 256
MOBA_BLOCK = 256
MOBA_TOPK = 3

LANES = 128
SUBLANES = 8
VMEM_LIMIT_BYTES = 56 * 1024 * 1024

MXU_DTYPE = jnp.bfloat16
NEG = -1e30
LOG2E = math.log2(math.e)

_NT = (((1,), (1,)), ((), ()))


def _cparams(sem):
    return pltpu.CompilerParams(dimension_semantics=sem, vmem_limit_bytes=VMEM_LIMIT_BYTES)


def _ada_kernel(ct_ref, w_ref, b_ref, o_ref):
    ct = ct_ref[...]
    ca = ct * (1.0 / (1.0 + jnp.exp(-ct)))
    o_ref[...] = jnp.sum(ca * w_ref[...], axis=0, keepdims=True) + b_ref[...]


def ada_modulation(c, ada_w, ada_b, tn=1024):
    depth, d, n = ada_w.shape
    ct = c.reshape(d, 1)
    return pl.pallas_call(
        _ada_kernel,
        grid=(depth, n // tn),
        in_specs=[
            pl.BlockSpec((d, 1), lambda l, j: (0, 0)),
            pl.BlockSpec((None, d, tn), lambda l, j: (l, 0, j)),
            pl.BlockSpec((None, 1, tn), lambda l, j: (l, 0, j)),
        ],
        out_specs=pl.BlockSpec((None, 1, tn), lambda l, j: (l, 0, j)),
        out_shape=jax.ShapeDtypeStruct((depth, 1, n), jnp.float32),
        compiler_params=_cparams(("parallel", "parallel")),
        name="ada_modulation",
    )(ct, ada_w, ada_b.reshape(depth, 1, n))


def _norm_mod(x, g, sc, sh):
    ms = jnp.mean(x * x, axis=-1, keepdims=True)
    y = x * lax.rsqrt(ms + EPS)
    return (y * g) * (1.0 + sc) + sh


def _nm_matmul_kernel(x_ref, g_ref, sc_ref, sh_ref, w_ref, o_ref, h_ref):
    @pl.when(pl.program_id(1) == 0)
    def _():
        h_ref[...] = _norm_mod(x_ref[...], g_ref[...], sc_ref[...], sh_ref[...]).astype(h_ref.dtype)

    o_ref[...] = jnp.dot(h_ref[...], w_ref[...].astype(MXU_DTYPE),
                         preferred_element_type=jnp.float32).astype(o_ref.dtype)


def norm_mod_matmul(x, g, sc, sh, w, layer, n_cols, out_dtype, tm=1024, tn=512):
    s, d = x.shape
    return pl.pallas_call(
        _nm_matmul_kernel,
        grid=(s // tm, n_cols // tn),
        in_specs=[
            pl.BlockSpec((tm, d), lambda i, j: (i, 0)),
            pl.BlockSpec((1, d), lambda i, j: (0, 0)),
            pl.BlockSpec((1, d), lambda i, j: (0, 0)),
            pl.BlockSpec((1, d), lambda i, j: (0, 0)),
            pl.BlockSpec((None, d, tn), lambda i, j: (layer, 0, j)),
        ],
        out_specs=pl.BlockSpec((tm, tn), lambda i, j: (i, j)),
        out_shape=jax.ShapeDtypeStruct((s, n_cols), out_dtype),
        scratch_shapes=[pltpu.VMEM((tm, d), MXU_DTYPE)],
        compiler_params=_cparams(("parallel", "arbitrary")),
        name="norm_mod_matmul",
    )(x, g, sc, sh, w)


def _nm_swiglu_kernel(x_ref, g_ref, sc_ref, sh_ref, wu_ref, wg_ref, o_ref, h_ref):
    @pl.when(pl.program_id(1) == 0)
    def _():
        h_ref[...] = _norm_mod(x_ref[...], g_ref[...], sc_ref[...], sh_ref[...]).astype(h_ref.dtype)

    h = h_ref[...]
    u = jnp.dot(h, wu_ref[...].astype(MXU_DTYPE), preferred_element_type=jnp.float32)
    gt = jnp.dot(h, wg_ref[...].astype(MXU_DTYPE), preferred_element_type=jnp.float32)
    o_ref[...] = ((gt * (1.0 / (1.0 + jnp.exp(-gt)))) * u).astype(o_ref.dtype)


def norm_mod_swiglu(x, g, sc, sh, w_in, layer, tm=1024, tn=512):
    s, d = x.shape
    f = w_in.shape[2] // 2
    nj = f // tn
    return pl.pallas_call(
        _nm_swiglu_kernel,
        grid=(s // tm, nj),
        in_specs=[
            pl.BlockSpec((tm, d), lambda i, j: (i, 0)),
            pl.BlockSpec((1, d), lambda i, j: (0, 0)),
            pl.BlockSpec((1, d), lambda i, j: (0, 0)),
            pl.BlockSpec((1, d), lambda i, j: (0, 0)),
            pl.BlockSpec((None, d, tn), lambda i, j: (layer, 0, j)),
            pl.BlockSpec((None, d, tn), lambda i, j: (layer, 0, j + nj)),
        ],
        out_specs=pl.BlockSpec((tm, tn), lambda i, j: (i, j)),
        out_shape=jax.ShapeDtypeStruct((s, f), MXU_DTYPE),
        scratch_shapes=[pltpu.VMEM((tm, d), MXU_DTYPE)],
        compiler_params=_cparams(("parallel", "arbitrary")),
        name="norm_mod_swiglu",
    )(x, g, sc, sh, w_in, w_in)


def _mm_postnorm_kernel(a_ref, w_ref, x_ref, g_ref, gate_ref, o_ref, acc_ref):
    k = pl.program_id(1)

    @pl.when(k == 0)
    def _():
        acc_ref[...] = jnp.zeros_like(acc_ref)

    acc_ref[...] += jnp.dot(a_ref[...], w_ref[...].astype(MXU_DTYPE), preferred_element_type=jnp.float32)

    @pl.when(k == pl.num_programs(1) - 1)
    def _():
        y = acc_ref[...]
        ms = jnp.mean(y * y, axis=-1, keepdims=True)
        yn = (y * lax.rsqrt(ms + EPS)) * g_ref[...]
        o_ref[...] = x_ref[...] + gate_ref[...] * yn


def matmul_postnorm_residual(a, w, layer, x, g, gate, tm=512, tk=512):
    s, kdim = a.shape
    d = w.shape[2]
    return pl.pallas_call(
        _mm_postnorm_kernel,
        grid=(s // tm, kdim // tk),
        in_specs=[
            pl.BlockSpec((tm, tk), lambda i, k: (i, k)),
            pl.BlockSpec((None, tk, d), lambda i, k: (layer, k, 0)),
            pl.BlockSpec((tm, d), lambda i, k: (i, 0)),
            pl.BlockSpec((1, d), lambda i, k: (0, 0)),
            pl.BlockSpec((1, d), lambda i, k: (0, 0)),
        ],
        out_specs=pl.BlockSpec((tm, d), lambda i, k: (i, 0)),
        out_shape=jax.ShapeDtypeStruct((s, d), jnp.float32),
        scratch_shapes=[pltpu.VMEM((tm, d), jnp.float32)],
        compiler_params=_cparams(("parallel", "arbitrary")),
        name="matmul_postnorm_residual",
    )(a, w, x, g, gate)


def rope_tables(positions, rot_dim, period):
    half = rot_dim // 2
    inv_freq = ROPE_THETA ** (-jnp.arange(half, dtype=jnp.float32) / half)
    ang = positions.astype(jnp.float32)[:, None] * inv_freq
    cos, sin = jnp.cos(ang), jnp.sin(ang)
    s = positions.shape[0]
    one = jnp.ones((s, period - rot_dim), jnp.float32)
    zero = jnp.zeros((s, period - rot_dim), jnp.float32)
    zh = jnp.zeros((s, half), jnp.float32)
    reps = LANES // period
    c = jnp.tile(jnp.concatenate([cos, cos, one], axis=1), (1, reps))
    sa = jnp.tile(jnp.concatenate([-sin, zh, zero], axis=1), (1, reps))
    sb = jnp.tile(jnp.concatenate([zh, sin, zero], axis=1), (1, reps))
    return c, sa, sb


def _rope_lanes(x, c, sa, sb, half):
    return x * c + pltpu.roll(x, LANES - half, 1) * sa + pltpu.roll(x, half, 1) * sb


def _rope_kernel(y_ref, c_ref, sa_ref, sb_ref, o_ref, *, half, out_scale, n_groups):
    c, sa, sb = c_ref[...], sa_ref[...], sb_ref[...]
    for h in range(n_groups):
        sl = slice(h * LANES, (h + 1) * LANES)
        r = _rope_lanes(y_ref[:, sl], c, sa, sb, half)
        o_ref[:, sl] = (r * out_scale).astype(o_ref.dtype)


def _rope_kmean_kernel(y_ref, c_ref, sa_ref, sb_ref, o_ref, km_ref, *, half, n_groups):
    c, sa, sb = c_ref[...], sa_ref[...], sb_ref[...]
    for h in range(n_groups):
        sl = slice(h * LANES, (h + 1) * LANES)
        r = _rope_lanes(y_ref[:, sl], c, sa, sb, half)
        o_ref[:, sl] = r.astype(o_ref.dtype)
        km_ref[:, sl] = jnp.mean(r, axis=0, keepdims=True)


def rope_cast(y, tables, col_start, n_cols, half, out_scale=1.0, tm=256):
    s = y.shape[0]
    cb = col_start // n_cols
    assert cb * n_cols == col_start
    tab = pl.BlockSpec((tm, LANES), lambda i: (i, 0))
    return pl.pallas_call(
        functools.partial(_rope_kernel, half=half, out_scale=out_scale, n_groups=n_cols // LANES),
        grid=(s // tm,),
        in_specs=[pl.BlockSpec((tm, n_cols), lambda i: (i, cb)), tab, tab, tab],
        out_specs=pl.BlockSpec((tm, n_cols), lambda i: (i, 0)),
        out_shape=jax.ShapeDtypeStruct((s, n_cols), MXU_DTYPE),
        compiler_params=_cparams(("parallel",)),
        name="rope_cast",
    )(y, *tables)


def _transpose_cast_kernel(y_ref, o_ref, *, n_groups):
    for h in range(n_groups):
        o_ref[h * LANES:(h + 1) * LANES, :] = y_ref[:, h * LANES:(h + 1) * LANES].T.astype(o_ref.dtype)


def transpose_cast_cols(y, col_start, n_cols, tm=512):
    s = y.shape[0]
    cb = col_start // n_cols
    assert cb * n_cols == col_start
    return pl.pallas_call(
        functools.partial(_transpose_cast_kernel, n_groups=n_cols // LANES),
        grid=(s // tm,),
        in_specs=[pl.BlockSpec((tm, n_cols), lambda i: (i, cb))],
        out_specs=pl.BlockSpec((n_cols, tm), lambda i: (0, i)),
        out_shape=jax.ShapeDtypeStruct((n_cols, s), MXU_DTYPE),
        compiler_params=_cparams(("parallel",)),
        name="transpose_cast_cols",
    )(y)


def rope_cast_kmean(y, tables, col_start, n_cols, half):
    s = y.shape[0]
    tm = MOBA_BLOCK
    cb = col_start // n_cols
    assert cb * n_cols == col_start
    tab = pl.BlockSpec((tm, LANES), lambda i: (i, 0))
    k, km = pl.pallas_call(
        functools.partial(_rope_kmean_kernel, half=half, n_groups=n_cols // LANES),
        grid=(s // tm,),
        in_specs=[pl.BlockSpec((tm, n_cols), lambda i: (i, cb)), tab, tab, tab],
        out_specs=[pl.BlockSpec((tm, n_cols), lambda i: (i, 0)),
                   pl.BlockSpec((None, 1, n_cols), lambda i: (i, 0, 0))],
        out_shape=[jax.ShapeDtypeStruct((s, n_cols), MXU_DTYPE),
                   jax.ShapeDtypeStruct((s // tm, 1, n_cols), jnp.float32)],
        compiler_params=_cparams(("parallel",)),
        name="rope_cast_kmean",
    )(y, *tables)
    return k, km.reshape(s // tm, n_cols)


def _idx_tail_kernel(y_ref, c_ref, sa_ref, sb_ref, kid_ref, wit_ref, *, half, w_scale):
    y = y_ref[...]
    r = _rope_lanes(y, c_ref[...], sa_ref[...], sb_ref[...], half)
    lane = lax.broadcasted_iota(jnp.int32, y.shape, 1)
    kid_ref[...] = jnp.where(lane < IDX_DIM, r, pltpu.roll(r, IDX_DIM, 1)).astype(kid_ref.dtype)
    wit_ref[...] = (y * w_scale).T


def idx_tail(y, tables, tm=512):
    s = y.shape[0]
    blk = pl.BlockSpec((tm, LANES), lambda i: (i, 0))
    return pl.pallas_call(
        functools.partial(_idx_tail_kernel, half=IDX_ROT_DIM // 2,
                          w_scale=IDX_HEADS ** -0.5 * IDX_DIM ** -0.5),
        grid=(s // tm,),
        in_specs=[blk, blk, blk, blk],
        out_specs=[blk, pl.BlockSpec((LANES, tm), lambda i: (0, i))],
        out_shape=[jax.ShapeDtypeStruct((s, LANES), MXU_DTYPE),
                   jax.ShapeDtypeStruct((LANES, s), jnp.float32)],
        compiler_params=_cparams(("parallel",)),
        name="idx_tail",
    )(y, *tables)


def _sortable_key(x):
    b = pltpu.bitcast(x, jnp.int32)
    return jnp.where(b < 0, b ^ jnp.int32(0x7FFFFFFF), b)


def _indexer_kernel(qi_ref, kid_ref, wit_ref, bias_ref, key_ref, qm_ref, *, tq, tk, n_sel):
    i = pl.program_id(0)
    s = bias_ref.shape[0]
    n_live = (i * tq + tq + tk - 1) // tk
    krow = lax.broadcasted_iota(jnp.int32, (tk, tq), 0)
    qcol = i * tq + lax.broadcasted_iota(jnp.int32, (tk, tq), 1)
    lane = lax.broadcasted_iota(jnp.int32, (tq, LANES), 1)

    for p in range(IDX_HEADS // 2):
        qp = qi_ref[:, p * LANES:(p + 1) * LANES]
        zero = jnp.zeros_like(qp)
        qm_ref[2 * p] = jnp.where(lane < IDX_DIM, qp, zero)
        qm_ref[2 * p + 1] = jnp.where(lane >= IDX_DIM, qp, zero)

    def score_tile(c, carry):
        k0 = pl.multiple_of(c * tk, tk)
        kc = kid_ref[pl.ds(k0, tk), :]
        acc = jnp.zeros((tk, tq), jnp.float32)
        for h in range(IDX_HEADS):
            d = lax.dot_general(kc, qm_ref[h], _NT, preferred_element_type=jnp.float32)
            acc = acc + jnp.maximum(d, 0.0) * wit_ref[IDX_DIM + h:IDX_DIM + h + 1, :]
        causal = (k0 + krow) <= qcol
        key_ref[pl.ds(k0, tk), :] = _sortable_key(jnp.where(causal, acc, -jnp.inf))
        return carry

    lax.fori_loop(0, n_live, score_tile, 0)

    def count_ge(cand):
        def body(c, cnt):
            k0 = pl.multiple_of(c * tk, tk)
            ind = jnp.where(key_ref[pl.ds(k0, tk), :] >= cand, 1.0, 0.0)
            return cnt + jnp.sum(ind.reshape(tk // SUBLANES, SUBLANES, tq), axis=0)
        cnt = lax.fori_loop(0, n_live, body, jnp.zeros((SUBLANES, tq), jnp.float32))
        return jnp.sum(cnt, axis=0, keepdims=True)

    need = float(n_sel)
    tau0 = jnp.full((1, tq), -2 ** 31, jnp.int32)
    zero_c = jnp.zeros((1, tq), jnp.int32)
    tau = jnp.where(count_ge(zero_c) >= need, zero_c, tau0)

    def bit_step(b, tau):
        cand = tau + lax.shift_left(jnp.int32(1), 30 - b)
        return jnp.where(count_ge(cand) >= need, cand, tau)

    tau = lax.fori_loop(0, 31, bit_step, tau)

    def write_live(c, carry):
        k0 = pl.multiple_of(c * tk, tk)
        causal = (k0 + krow) <= qcol
        sel = jnp.where(key_ref[pl.ds(k0, tk), :] >= tau, 0.0, NEG)
        bias_ref[pl.ds(k0, tk), :] = jnp.where(causal, sel, NEG).astype(bias_ref.dtype)
        return carry

    lax.fori_loop(0, n_live, write_live, 0)

    def write_dead(c, carry):
        k0 = pl.multiple_of(c * tk, tk)
        bias_ref[pl.ds(k0, tk), :] = jnp.full((tk, tq), NEG, bias_ref.dtype)
        return carry

    lax.fori_loop(n_live, s // tk, write_dead, 0)


def dsa_index_bias(qi, kid, wit, n_sel, tq=256, tk=512):
    s = qi.shape[0]
    return pl.pallas_call(
        functools.partial(_indexer_kernel, tq=tq, tk=tk, n_sel=n_sel),
        grid=(s // tq,),
        in_specs=[
            pl.BlockSpec((tq, IDX_HEADS * IDX_DIM), lambda i: (i, 0)),
            pl.BlockSpec((s, LANES), lambda i: (0, 0)),
            pl.BlockSpec((LANES, tq), lambda i: (0, i)),
        ],
        out_specs=pl.BlockSpec((None, s, tq), lambda i: (i, 0, 0)),
        out_shape=jax.ShapeDtypeStruct((s // tq, s, tq), MXU_DTYPE),
        scratch_shapes=[pltpu.VMEM((s, tq), jnp.int32),
                        pltpu.VMEM((IDX_HEADS, tq, LANES), MXU_DTYPE)],
        compiler_params=_cparams(("parallel",)),
        name="dsa_index_bias",
    )(qi, kid, wit)


def _flash_update(h, s_t, v_t, m_ref, l_ref, acc_ref):
    m_prev = m_ref[h]
    m_new = jnp.maximum(m_prev, jnp.max(s_t, axis=0, keepdims=True))
    alpha = jnp.exp2(m_prev - m_new)
    p_t = jnp.exp2(s_t - m_new)
    l_ref[h] = alpha * l_ref[h] + jnp.sum(p_t, axis=0, keepdims=True)
    acc_ref[h] = alpha * acc_ref[h] + jnp.dot(v_t, p_t.astype(MXU_DTYPE), preferred_element_type=jnp.float32)
    m_ref[h] = m_new


def _flash_init(m_ref, l_ref, acc_ref):
    m_ref[...] = jnp.full(m_ref.shape, NEG, jnp.float32)
    l_ref[...] = jnp.zeros_like(l_ref)
    acc_ref[...] = jnp.zeros_like(acc_ref)


def _flash_finish(o_ref, l_ref, acc_ref):
    for h in range(N_HEADS):
        o_ref[:, h * HEAD_DIM:(h + 1) * HEAD_DIM] = (acc_ref[h] / l_ref[h]).T.astype(o_ref.dtype)


def _flash_scratch(tq):
    return [pltpu.VMEM((N_HEADS, 1, tq), jnp.float32),
            pltpu.VMEM((N_HEADS, 1, tq), jnp.float32),
            pltpu.VMEM((N_HEADS, HEAD_DIM, tq), jnp.float32)]


def _dsa_attn_kernel(q_ref, k_ref, vt_ref, bias_ref, o_ref, m_ref, l_ref, acc_ref, *, tq, tk):
    i, j = pl.program_id(0), pl.program_id(1)

    @pl.when(j == 0)
    def _():
        _flash_init(m_ref, l_ref, acc_ref)

    @pl.when(j * tk < (i + 1) * tq)
    def _():
        bias = bias_ref[...].astype(jnp.float32)
        rep = N_HEADS // A_KV_HEADS
        for g in range(A_KV_HEADS):
            kg = k_ref[:, g * HEAD_DIM:(g + 1) * HEAD_DIM]
            vg = vt_ref[g * HEAD_DIM:(g + 1) * HEAD_DIM, :]
            for r in range(rep):
                h = g * rep + r
                qh = q_ref[:, h * HEAD_DIM:(h + 1) * HEAD_DIM]
                s_t = lax.dot_general(kg, qh, _NT, preferred_element_type=jnp.float32) + bias
                _flash_update(h, s_t, vg, m_ref, l_ref, acc_ref)

    @pl.when(j == pl.num_programs(1) - 1)
    def _():
        _flash_finish(o_ref, l_ref, acc_ref)


def dsa_attention(q, k, v_t, bias, tq=256, tk=1024):
    s = q.shape[0]
    tk = min(tk, s)
    kvw = A_KV_HEADS * HEAD_DIM

    def live_j(i, j):
        return jnp.minimum(j, ((i + 1) * tq - 1) // tk)

    return pl.pallas_call(
        functools.partial(_dsa_attn_kernel, tq=tq, tk=tk),
        grid=(s // tq, s // tk),
        in_specs=[
            pl.BlockSpec((tq, N_HEADS * HEAD_DIM), lambda i, j: (i, 0)),
            pl.BlockSpec((tk, kvw), lambda i, j: (live_j(i, j), 0)),
            pl.BlockSpec((kvw, tk), lambda i, j: (0, live_j(i, j))),
            pl.BlockSpec((None, tk, tq), lambda i, j: (i, live_j(i, j), 0)),
        ],
        out_specs=pl.BlockSpec((tq, N_HEADS * HEAD_DIM), lambda i, j: (i, 0)),
        out_shape=jax.ShapeDtypeStruct((s, N_HEADS * HEAD_DIM), MXU_DTYPE),
        scratch_shapes=_flash_scratch(tq),
        compiler_params=_cparams(("parallel", "arbitrary")),
        name="dsa_attention",
    )(q, k, v_t, bias)


def _moba_attn_kernel(q_ref, k_ref, vt_ref, km_ref, o_ref, m_ref, l_ref, acc_ref, sel_ref, *, n_sel, bps):
    i, j = pl.program_id(0), pl.program_id(1)
    t = MOBA_BLOCK
    blk = lax.broadcasted_iota(jnp.int32, (LANES, t), 0)
    blk_f = blk.astype(jnp.float32)

    @pl.when(j == 0)
    def _():
        _flash_init(m_ref, l_ref, acc_ref)
        for h in range(N_HEADS):
            sl = slice(h * HEAD_DIM, (h + 1) * HEAD_DIM)
            gate = lax.dot_general(km_ref[:, sl].astype(MXU_DTYPE), q_ref[:, sl], _NT,
                                   preferred_element_type=jnp.float32)
            gate = jnp.where(blk < i, gate, -jnp.inf)
            bias = jnp.full((LANES, t), NEG, jnp.float32)
            for _ in range(n_sel):
                top = jnp.max(gate, axis=0, keepdims=True)
                first = jnp.min(jnp.where(gate == top, blk_f, float(LANES)), axis=0, keepdims=True)
                hit = blk_f == first
                bias = jnp.where(hit, 0.0, bias)
                gate = jnp.where(hit, -jnp.inf, gate)
            sel_ref[h] = jnp.where(blk < i, bias, NEG)

    def step(on_diagonal):
        kpos = lax.broadcasted_iota(jnp.int32, (t, t), 0)
        qpos = lax.broadcasted_iota(jnp.int32, (t, t), 1)
        causal = jnp.where(kpos <= qpos, 0.0, NEG)
        for h in range(N_HEADS):
            sl = slice(h * HEAD_DIM, (h + 1) * HEAD_DIM)
            s_t = lax.dot_general(k_ref[:, sl], q_ref[:, sl], _NT, preferred_element_type=jnp.float32)
            parts = []
            for b in range(bps):
                kb = j * bps + b
                bias_b = sel_ref[h, pl.ds(kb, 1), :]
                if on_diagonal:
                    own = jnp.where(kb == i, 1.0, 0.0)
                    bias_b = jnp.maximum(bias_b, causal * own + NEG * (1.0 - own))
                parts.append(s_t[b * t:(b + 1) * t, :] + bias_b)
            _flash_update(h, jnp.concatenate(parts, axis=0), vt_ref[sl, :], m_ref, l_ref, acc_ref)

    @pl.when((j + 1) * bps <= i)
    def _():
        step(False)

    @pl.when(jnp.logical_and(j * bps <= i, (j + 1) * bps > i))
    def _():
        step(True)

    @pl.when(j == pl.num_programs(1) - 1)
    def _():
        _flash_finish(o_ref, l_ref, acc_ref)


def moba_attention(q, k, v_t, kmean_pad, n_sel, bps=4):
    s, hd = q.shape
    t = MOBA_BLOCK
    nb = s // t
    bps = min(bps, nb)
    tk = bps * t

    def live_j(i, j):
        return jnp.minimum(j, i // bps)

    return pl.pallas_call(
        functools.partial(_moba_attn_kernel, n_sel=n_sel, bps=bps),
        grid=(nb, nb // bps),
        in_specs=[
            pl.BlockSpec((t, hd), lambda i, j: (i, 0)),
            pl.BlockSpec((tk, hd), lambda i, j: (live_j(i, j), 0)),
            pl.BlockSpec((hd, tk), lambda i, j: (0, live_j(i, j))),
            pl.BlockSpec((LANES, hd), lambda i, j: (0, 0)),
        ],
        out_specs=pl.BlockSpec((t, hd), lambda i, j: (i, 0)),
        out_shape=jax.ShapeDtypeStruct((s, hd), MXU_DTYPE),
        scratch_shapes=_flash_scratch(t) + [pltpu.VMEM((N_HEADS, LANES, t), jnp.float32)],
        compiler_params=_cparams(("parallel", "arbitrary")),
        name="moba_attention",
    )(q, k, v_t, kmean_pad)


def _dsa_mixer(x, g, sc, sh, w_in, layer, tabs_main, tabs_idx):
    s = x.shape[0]
    hd = N_HEADS * HEAD_DIM
    kvw = A_KV_HEADS * HEAD_DIM
    n_main = hd + 2 * kvw + IDX_HEADS * IDX_DIM
    n_tail = w_in.shape[2] - n_main
    y = norm_mod_matmul(x, g, sc, sh, w_in, layer, n_main, jnp.float32)
    w_tail = jnp.pad(w_in[layer, :, n_main:], ((0, 0), (0, LANES - n_tail)))[None]
    y_tail = norm_mod_matmul(x, g, sc, sh, w_tail, 0, LANES, jnp.float32, tn=LANES)
    q = rope_cast(y, tabs_main, 0, hd, ROT_DIM // 2, out_scale=HEAD_DIM ** -0.5 * LOG2E)
    k = rope_cast(y, tabs_main, hd, kvw, ROT_DIM // 2)
    v_t = transpose_cast_cols(y, hd + kvw, kvw)
    qi = rope_cast(y, tabs_idx, hd + 2 * kvw, IDX_HEADS * IDX_DIM, IDX_ROT_DIM // 2)
    kid, wit = idx_tail(y_tail, tabs_idx)
    n_sel = min(IDX_TOPK_MAX, s // 4)
    bias = dsa_index_bias(qi, kid, wit, n_sel)
    return dsa_attention(q, k, v_t, bias)


def _moba_mixer(x, g, sc, sh, w_in, layer, tabs_main):
    s = x.shape[0]
    hd = N_HEADS * HEAD_DIM
    y = norm_mod_matmul(x, g, sc, sh, w_in, layer, 3 * hd, jnp.float32)
    q = rope_cast(y, tabs_main, 0, hd, ROT_DIM // 2, out_scale=HEAD_DIM ** -0.5 * LOG2E)
    k, kmean = rope_cast_kmean(y, tabs_main, hd, hd, ROT_DIM // 2)
    v_t = transpose_cast_cols(y, 2 * hd, hd)
    nb = s // MOBA_BLOCK
    kmean_pad = jnp.pad(kmean, ((0, LANES - nb), (0, 0)))
    return moba_attention(q, k, v_t, kmean_pad, min(MOBA_TOPK, nb))


def kernel(x, c, positions, a_w_in, a_w_o, b_w_in, b_w_o, ada_w, ada_b, norm_g, ffn_w_in, ffn_w_out):
    b, s, d = x.shape
    assert b == 1 and s % MOBA_BLOCK == 0 and s // MOBA_BLOCK <= LANES
    depth = ada_w.shape[0]
    xs = x[0]
    mod = ada_modulation(c, ada_w, ada_b)
    pos = positions[0]
    tabs_main = rope_tables(pos, ROT_DIM, HEAD_DIM)
    tabs_idx = rope_tables(pos, IDX_ROT_DIM, IDX_DIM)
    for i in range(depth):
        sh1, sc1, g1, sh2, sc2, g2 = [mod[i, :, t * d:(t + 1) * d] for t in range(6)]
        ng = [norm_g[i, t][None, :] for t in range(4)]
        if i % 2 == 0:
            o = _dsa_mixer(xs, ng[0], sc1, sh1, a_w_in, i // 2, tabs_main, tabs_idx)
            w_o = a_w_o
        else:
            o = _moba_mixer(xs, ng[0], sc1, sh1, b_w_in, i // 2, tabs_main)
            w_o = b_w_o
        xs = matmul_postnorm_residual(o, w_o, i // 2, xs, ng[1], g1)
        act = norm_mod_swiglu(xs, ng[2], sc2, sh2, ffn_w_in, i)
        xs = matmul_postnorm_residual(act, ffn_w_out, i, xs, ng[3], g2)
    return xs[None]
```

```python
import functools
import math

import jax
import jax.numpy as jnp
from jax import lax
from jax.experimental import pallas as pl
from jax.experimental.pallas import tpu as pltpu

N_HEADS = 16
HEAD_DIM = 128
ROT_DIM = HEAD_DIM // 4
ROPE_THETA = 500000.0
EPS = 1e-6
A_KV_HEADS = 4
IDX_HEADS = 16
IDX_DIM = 64
IDX_ROT_DIM = IDX_DIM // 4
IDX_TOPK_MAX = 256
MOBA_BLOCK = 256
MOBA_TOPK = 3

LANES = 128
SUBLANES = 8
VMEM_LIMIT_BYTES = 56 * 1024 * 1024

MXU_DTYPE = jnp.bfloat16
BF16_SUBLANES = 16
V_ROWS = HEAD_DIM + BF16_SUBLANES
NEG = -1e30
LOG2E = math.log2(math.e)

_NT = (((1,), (1,)), ((), ()))


def _cparams(sem):
    return pltpu.CompilerParams(dimension_semantics=sem, vmem_limit_bytes=VMEM_LIMIT_BYTES)


def _ada_kernel(ct_ref, w_ref, b_ref, o_ref):
    ct = ct_ref[...]
    ca = ct * (1.0 / (1.0 + jnp.exp(-ct)))
    o_ref[...] = jnp.sum(ca * w_ref[...], axis=0, keepdims=True) + b_ref[...]


def ada_modulation(c, ada_w, ada_b, tn=1024):
    depth, d, n = ada_w.shape
    ct = c.reshape(d, 1)
    return pl.pallas_call(
        _ada_kernel,
        grid=(depth, n // tn),
        in_specs=[
            pl.BlockSpec((d, 1), lambda l, j: (0, 0)),
            pl.BlockSpec((None, d, tn), lambda l, j: (l, 0, j)),
            pl.BlockSpec((None, 1, tn), lambda l, j: (l, 0, j)),
        ],
        out_specs=pl.BlockSpec((None, 1, tn), lambda l, j: (l, 0, j)),
        out_shape=jax.ShapeDtypeStruct((depth, 1, n), jnp.float32),
        compiler_params=_cparams(("parallel", "parallel")),
        name="ada_modulation",
    )(ct, ada_w, ada_b.reshape(depth, 1, n))


def _norm_mod(x, g, sc, sh):
    ms = jnp.mean(x * x, axis=-1, keepdims=True)
    y = x * lax.rsqrt(ms + EPS)
    return (y * g) * (1.0 + sc) + sh


def _nm_matmul_kernel(x_ref, g_ref, sc_ref, sh_ref, w_ref, o_ref, h_ref):
    @pl.when(pl.program_id(1) == 0)
    def _():
        h_ref[...] = _norm_mod(x_ref[...], g_ref[...], sc_ref[...], sh_ref[...]).astype(h_ref.dtype)

    o_ref[...] = jnp.dot(h_ref[...], w_ref[...].astype(MXU_DTYPE),
                         preferred_element_type=jnp.float32).astype(o_ref.dtype)


def norm_mod_matmul(x, g, sc, sh, w, layer, n_cols, out_dtype, tm=1024, tn=512):
    s, d = x.shape
    return pl.pallas_call(
        _nm_matmul_kernel,
        grid=(s // tm, n_cols // tn),
        in_specs=[
            pl.BlockSpec((tm, d), lambda i, j: (i, 0)),
            pl.BlockSpec((1, d), lambda i, j: (0, 0)),
            pl.BlockSpec((1, d), lambda i, j: (0, 0)),
            pl.BlockSpec((1, d), lambda i, j: (0, 0)),
            pl.BlockSpec((None, d, tn), lambda i, j: (layer, 0, j)),
        ],
        out_specs=pl.BlockSpec((tm, tn), lambda i, j: (i, j)),
        out_shape=jax.ShapeDtypeStruct((s, n_cols), out_dtype),
        scratch_shapes=[pltpu.VMEM((tm, d), MXU_DTYPE)],
        compiler_params=_cparams(("parallel", "arbitrary")),
        name="norm_mod_matmul",
    )(x, g, sc, sh, w)


def _nm_swiglu_kernel(x_ref, g_ref, sc_ref, sh_ref, wu_ref, wg_ref, o_ref, h_ref):
    @pl.when(pl.program_id(1) == 0)
    def _():
        h_ref[...] = _norm_mod(x_ref[...], g_ref[...], sc_ref[...], sh_ref[...]).astype(h_ref.dtype)

    h = h_ref[...]
    u = jnp.dot(h, wu_ref[...].astype(MXU_DTYPE), preferred_element_type=jnp.float32)
    gt = jnp.dot(h, wg_ref[...].astype(MXU_DTYPE), preferred_element_type=jnp.float32)
    o_ref[...] = ((gt * (1.0 / (1.0 + jnp.exp(-gt)))) * u).astype(o_ref.dtype)


def norm_mod_swiglu(x, g, sc, sh, w_in, layer, tm=1024, tn=512):
    s, d = x.shape
    f = w_in.shape[2] // 2
    nj = f // tn
    return pl.pallas_call(
        _nm_swiglu_kernel,
        grid=(s // tm, nj),
        in_specs=[
            pl.BlockSpec((tm, d), lambda i, j: (i, 0)),
            pl.BlockSpec((1, d), lambda i, j: (0, 0)),
            pl.BlockSpec((1, d), lambda i, j: (0, 0)),
            pl.BlockSpec((1, d), lambda i, j: (0, 0)),
            pl.BlockSpec((None, d, tn), lambda i, j: (layer, 0, j)),
            pl.BlockSpec((None, d, tn), lambda i, j: (layer, 0, j + nj)),
        ],
        out_specs=pl.BlockSpec((tm, tn), lambda i, j: (i, j)),
        out_shape=jax.ShapeDtypeStruct((s, f), MXU_DTYPE),
        scratch_shapes=[pltpu.VMEM((tm, d), MXU_DTYPE)],
        compiler_params=_cparams(("parallel", "arbitrary")),
        name="norm_mod_swiglu",
    )(x, g, sc, sh, w_in, w_in)


def _mm_postnorm_kernel(a_ref, w_ref, x_ref, g_ref, gate_ref, o_ref):
    k = pl.program_id(1)

    @pl.when(k == 0)
    def _():
        o_ref[...] = jnp.zeros_like(o_ref)

    o_ref[...] += jnp.dot(a_ref[...], w_ref[...].astype(MXU_DTYPE), preferred_element_type=jnp.float32)

    @pl.when(k == pl.num_programs(1) - 1)
    def _():
        y = o_ref[...]
        ms = jnp.mean(y * y, axis=-1, keepdims=True)
        yn = (y * lax.rsqrt(ms + EPS)) * g_ref[...]
        o_ref[...] = x_ref[...] + gate_ref[...] * yn


def matmul_postnorm_residual(a, w, layer, x, g, gate, tm=1024, tk=512):
    s, kdim = a.shape
    d = w.shape[2]
    return pl.pallas_call(
        _mm_postnorm_kernel,
        grid=(s // tm, kdim // tk),
        in_specs=[
            pl.BlockSpec((tm, tk), lambda i, k: (i, k)),
            pl.BlockSpec((None, tk, d), lambda i, k: (layer, k, 0)),
            pl.BlockSpec((tm, d), lambda i, k: (i, 0), pipeline_mode=pl.Buffered(1)),
            pl.BlockSpec((1, d), lambda i, k: (0, 0)),
            pl.BlockSpec((1, d), lambda i, k: (0, 0)),
        ],
        out_specs=pl.BlockSpec((tm, d), lambda i, k: (i, 0)),
        out_shape=jax.ShapeDtypeStruct((s, d), jnp.float32),
        compiler_params=_cparams(("parallel", "arbitrary")),
        name="matmul_postnorm_residual",
    )(a, w, x, g, gate)


def rope_tables(positions, rot_dim, period):
    half = rot_dim // 2
    inv_freq = ROPE_THETA ** (-jnp.arange(half, dtype=jnp.float32) / half)
    ang = positions.astype(jnp.float32)[:, None] * inv_freq
    cos, sin = jnp.cos(ang), jnp.sin(ang)
    s = positions.shape[0]
    one = jnp.ones((s, period - rot_dim), jnp.float32)
    zero = jnp.zeros((s, period - rot_dim), jnp.float32)
    zh = jnp.zeros((s, half), jnp.float32)
    reps = LANES // period
    c = jnp.tile(jnp.concatenate([cos, cos, one], axis=1), (1, reps))
    sa = jnp.tile(jnp.concatenate([-sin, zh, zero], axis=1), (1, reps))
    sb = jnp.tile(jnp.concatenate([zh, sin, zero], axis=1), (1, reps))
    return c, sa, sb


def _rope_lanes(x, c, sa, sb, half):
    return x * c + pltpu.roll(x, LANES - half, 1) * sa + pltpu.roll(x, half, 1) * sb


def _rope_kernel(y_ref, c_ref, sa_ref, sb_ref, o_ref, *, half, out_scale, n_groups):
    c, sa, sb = c_ref[...], sa_ref[...], sb_ref[...]
    for h in range(n_groups):
        sl = slice(h * LANES, (h + 1) * LANES)
        r = _rope_lanes(y_ref[:, sl], c, sa, sb, half)
        o_ref[:, sl] = (r * out_scale).astype(o_ref.dtype)


def _rope_kmean_kernel(y_ref, c_ref, sa_ref, sb_ref, o_ref, km_ref, *, half, n_groups):
    c, sa, sb = c_ref[...], sa_ref[...], sb_ref[...]
    for h in range(n_groups):
        sl = slice(h * LANES, (h + 1) * LANES)
        r = _rope_lanes(y_ref[:, sl], c, sa, sb, half)
        o_ref[:, sl] = r.astype(o_ref.dtype)
        km_ref[:, sl] = jnp.mean(r, axis=0, keepdims=True)


def rope_cast(y, tables, col_start, n_cols, half, out_scale=1.0, tm=256):
    s = y.shape[0]
    cb = col_start // n_cols
    assert cb * n_cols == col_start
    tab = pl.BlockSpec((tm, LANES), lambda i: (i, 0))
    return pl.pallas_call(
        functools.partial(_rope_kernel, half=half, out_scale=out_scale, n_groups=n_cols // LANES),
        grid=(s // tm,),
        in_specs=[pl.BlockSpec((tm, n_cols), lambda i: (i, cb)), tab, tab, tab],
        out_specs=pl.BlockSpec((tm, n_cols), lambda i: (i, 0)),
        out_shape=jax.ShapeDtypeStruct((s, n_cols), MXU_DTYPE),
        compiler_params=_cparams(("parallel",)),
        name="rope_cast",
    )(y, *tables)


def _transpose_values_kernel(y_ref, o_ref, *, n_heads):
    tm = y_ref.shape[0]
    for h in range(n_heads):
        r0 = h * V_ROWS
        o_ref[r0:r0 + HEAD_DIM, :] = y_ref[:, h * HEAD_DIM:(h + 1) * HEAD_DIM].T.astype(o_ref.dtype)
        o_ref[r0 + HEAD_DIM:r0 + V_ROWS, :] = jnp.ones((V_ROWS - HEAD_DIM, tm), o_ref.dtype)


def transpose_values(y, col_start, n_heads, tm=512):
    s = y.shape[0]
    n_cols = n_heads * HEAD_DIM
    cb = col_start // n_cols
    assert cb * n_cols == col_start
    return pl.pallas_call(
        functools.partial(_transpose_values_kernel, n_heads=n_heads),
        grid=(s // tm,),
        in_specs=[pl.BlockSpec((tm, n_cols), lambda i: (i, cb))],
        out_specs=pl.BlockSpec((n_heads * V_ROWS, tm), lambda i: (0, i)),
        out_shape=jax.ShapeDtypeStruct((n_heads * V_ROWS, s), MXU_DTYPE),
        compiler_params=_cparams(("parallel",)),
        name="transpose_values",
    )(y)


def rope_cast_kmean(y, tables, col_start, n_cols, half):
    s = y.shape[0]
    tm = MOBA_BLOCK
    cb = col_start // n_cols
    assert cb * n_cols == col_start
    tab = pl.BlockSpec((tm, LANES), lambda i: (i, 0))
    k, km = pl.pallas_call(
        functools.partial(_rope_kmean_kernel, half=half, n_groups=n_cols // LANES),
        grid=(s // tm,),
        in_specs=[pl.BlockSpec((tm, n_cols), lambda i: (i, cb)), tab, tab, tab],
        out_specs=[pl.BlockSpec((tm, n_cols), lambda i: (i, 0)),
                   pl.BlockSpec((None, 1, n_cols), lambda i: (i, 0, 0))],
        out_shape=[jax.ShapeDtypeStruct((s, n_cols), MXU_DTYPE),
                   jax.ShapeDtypeStruct((s // tm, 1, n_cols), jnp.float32)],
        compiler_params=_cparams(("parallel",)),
        name="rope_cast_kmean",
    )(y, *tables)
    return k, km.reshape(s // tm, n_cols)


def _idx_tail_kernel(y_ref, c_ref, sa_ref, sb_ref, kid_ref, wit_ref, *, half, w_scale):
    y = y_ref[...]
    r = _rope_lanes(y, c_ref[...], sa_ref[...], sb_ref[...], half)
    lane = lax.broadcasted_iota(jnp.int32, y.shape, 1)
    kid_ref[...] = jnp.where(lane < IDX_DIM, r, pltpu.roll(r, IDX_DIM, 1)).astype(kid_ref.dtype)
    wit_ref[...] = (y * w_scale).T


def idx_tail(y, tables, tm=512):
    s = y.shape[0]
    blk = pl.BlockSpec((tm, LANES), lambda i: (i, 0))
    return pl.pallas_call(
        functools.partial(_idx_tail_kernel, half=IDX_ROT_DIM // 2,
                          w_scale=IDX_HEADS ** -0.5 * IDX_DIM ** -0.5),
        grid=(s // tm,),
        in_specs=[blk, blk, blk, blk],
        out_specs=[blk, pl.BlockSpec((LANES, tm), lambda i: (0, i))],
        out_shape=[jax.ShapeDtypeStruct((s, LANES), MXU_DTYPE),
                   jax.ShapeDtypeStruct((LANES, s), jnp.float32)],
        compiler_params=_cparams(("parallel",)),
        name="idx_tail",
    )(y, *tables)


def _sortable_key(x):
    b = pltpu.bitcast(x, jnp.int32)
    return jnp.where(b < 0, b ^ jnp.int32(0x7FFFFFFF), b)


def _indexer_kernel(qi_ref, kid_ref, wit_ref, bias_ref, key_ref, qm_ref, *, tq, tk, n_sel):
    i = pl.program_id(0)
    s = bias_ref.shape[0]
    n_live = (i * tq + tq + tk - 1) // tk
    krow = lax.broadcasted_iota(jnp.int32, (tk, tq), 0)
    qcol = i * tq + lax.broadcasted_iota(jnp.int32, (tk, tq), 1)
    lane = lax.broadcasted_iota(jnp.int32, (tq, LANES), 1)

    for p in range(IDX_HEADS // 2):
        qp = qi_ref[:, p * LANES:(p + 1) * LANES]
        zero = jnp.zeros_like(qp)
        qm_ref[2 * p] = jnp.where(lane < IDX_DIM, qp, zero)
        qm_ref[2 * p + 1] = jnp.where(lane >= IDX_DIM, qp, zero)

    def score_tile(c, carry):
        k0 = pl.multiple_of(c * tk, tk)
        kc = kid_ref[pl.ds(k0, tk), :]
        acc = jnp.zeros((tk, tq), jnp.float32)
        for h in range(IDX_HEADS):
            d = lax.dot_general(kc, qm_ref[h], _NT, preferred_element_type=jnp.float32)
            acc = acc + jnp.maximum(d, 0.0) * wit_ref[IDX_DIM + h:IDX_DIM + h + 1, :]
        causal = (k0 + krow) <= qcol
        key_ref[pl.ds(k0, tk), :] = _sortable_key(jnp.where(causal, acc, -jnp.inf))
        return carry

    lax.fori_loop(0, n_live, score_tile, 0)

    def count_ge(cand):
        rows = 2 * SUBLANES

        def body(c, cnt):
            k0 = pl.multiple_of(c * tk, tk)
            ind = jnp.where(key_ref[pl.ds(k0, tk), :] >= cand, 1.0, 0.0)
            return cnt + jnp.sum(ind.reshape(tk // rows, rows, tq), axis=0)
        cnt = lax.fori_loop(0, n_live, body, jnp.zeros((rows, tq), jnp.float32))
        return jnp.sum(cnt, axis=0, keepdims=True)

    need = float(n_sel)
    tau0 = jnp.full((1, tq), -2 ** 31, jnp.int32)
    zero_c = jnp.zeros((1, tq), jnp.int32)
    tau = jnp.where(count_ge(zero_c) >= need, zero_c, tau0)

    def bit_step(b, tau):
        cand = tau + lax.shift_left(jnp.int32(1), 30 - b)
        return jnp.where(count_ge(cand) >= need, cand, tau)

    tau = lax.fori_loop(0, 31, bit_step, tau)

    def write_live(c, carry):
        k0 = pl.multiple_of(c * tk, tk)
        causal = (k0 + krow) <= qcol
        sel = jnp.where(key_ref[pl.ds(k0, tk), :] >= tau, 0.0, NEG)
        bias_ref[pl.ds(k0, tk), :] = jnp.where(causal, sel, NEG).astype(bias_ref.dtype)
        return carry

    lax.fori_loop(0, n_live, write_live, 0)

    def write_dead(c, carry):
        k0 = pl.multiple_of(c * tk, tk)
        bias_ref[pl.ds(k0, tk), :] = jnp.full((tk, tq), NEG, bias_ref.dtype)
        return carry

    lax.fori_loop(n_live, s // tk, write_dead, 0)


def dsa_index_bias(qi, kid, wit, n_sel, tq=256, tk=512):
    s = qi.shape[0]
    return pl.pallas_call(
        functools.partial(_indexer_kernel, tq=tq, tk=tk, n_sel=n_sel),
        grid=(s // tq,),
        in_specs=[
            pl.BlockSpec((tq, IDX_HEADS * IDX_DIM), lambda i: (i, 0)),
            pl.BlockSpec((s, LANES), lambda i: (0, 0)),
            pl.BlockSpec((LANES, tq), lambda i: (0, i)),
        ],
        out_specs=pl.BlockSpec((None, s, tq), lambda i: (i, 0, 0)),
        out_shape=jax.ShapeDtypeStruct((s // tq, s, tq), MXU_DTYPE),
        scratch_shapes=[pltpu.VMEM((s, tq), jnp.int32),
                        pltpu.VMEM((IDX_HEADS, tq, LANES), MXU_DTYPE)],
        compiler_params=_cparams(("parallel",)),
        name="dsa_index_bias",
    )(qi, kid, wit)


def _flash_stage_a(h, s_t, m_ref, al_ref, s_ref):
    m_prev = m_ref[h]
    m_new = jnp.maximum(m_prev, jnp.max(s_t, axis=0, keepdims=True))
    al_ref[h] = jnp.exp2(m_prev - m_new)
    m_ref[h] = m_new
    s_ref[h % 2] = s_t


def _flash_stage_b(h, v_aug, m_ref, al_ref, s_ref, acc_ref):
    p_t = jnp.exp2(s_ref[h % 2] - m_ref[h]).astype(MXU_DTYPE)
    acc_ref[h] = al_ref[h] * acc_ref[h] + jnp.dot(v_aug, p_t, preferred_element_type=jnp.float32)


def _flash_tile(scores, values, m_ref, al_ref, s_ref, acc_ref):
    _flash_stage_a(0, scores(0), m_ref, al_ref, s_ref)
    for h in range(N_HEADS):
        if h + 1 < N_HEADS:
            _flash_stage_a(h + 1, scores(h + 1), m_ref, al_ref, s_ref)
        _flash_stage_b(h, values(h), m_ref, al_ref, s_ref, acc_ref)


def _flash_init(m_ref, acc_ref):
    m_ref[...] = jnp.full(m_ref.shape, NEG, jnp.float32)
    acc_ref[...] = jnp.zeros_like(acc_ref)


def _flash_finish(o_ref, acc_ref):
    for h in range(N_HEADS):
        out_t = acc_ref[h, 0:HEAD_DIM, :] / acc_ref[h, HEAD_DIM:HEAD_DIM + 1, :]
        o_ref[:, h * HEAD_DIM:(h + 1) * HEAD_DIM] = out_t.T.astype(o_ref.dtype)


def _flash_scratch(tq, tk):
    return [pltpu.VMEM((N_HEADS, 1, tq), jnp.float32),
            pltpu.VMEM((N_HEADS, 1, tq), jnp.float32),
            pltpu.VMEM((2, tk, tq), jnp.float32),
            pltpu.VMEM((N_HEADS, V_ROWS, tq), jnp.float32)]


def _dsa_attn_kernel(q_ref, k_ref, vt_ref, bias_ref, o_ref, m_ref, al_ref, s_ref, acc_ref, bias_f32_ref,
                     *, tq, tk):
    i, j = pl.program_id(0), pl.program_id(1)
    rep = N_HEADS // A_KV_HEADS

    @pl.when(j == 0)
    def _():
        _flash_init(m_ref, acc_ref)

    @pl.when(j * tk < (i + 1) * tq)
    def _():
        bias_f32_ref[...] = bias_ref[...].astype(jnp.float32)

        def scores(h):
            g = h // rep
            kg = k_ref[:, g * HEAD_DIM:(g + 1) * HEAD_DIM]
            qh = q_ref[:, h * HEAD_DIM:(h + 1) * HEAD_DIM]
            return lax.dot_general(kg, qh, _NT, preferred_element_type=jnp.float32) + bias_f32_ref[...]

        def values(h):
            g = h // rep
            return vt_ref[g * V_ROWS:(g + 1) * V_ROWS, :]

        _flash_tile(scores, values, m_ref, al_ref, s_ref, acc_ref)

    @pl.when(j == pl.num_programs(1) - 1)
    def _():
        _flash_finish(o_ref, acc_ref)


def dsa_attention(q, k, v_t, bias, tq=256, tk=1024):
    s = q.shape[0]
    tk = min(tk, s)
    kvw = A_KV_HEADS * HEAD_DIM

    def live_j(i, j):
        return jnp.minimum(j, ((i + 1) * tq - 1) // tk)

    return pl.pallas_call(
        functools.partial(_dsa_attn_kernel, tq=tq, tk=tk),
        grid=(s // tq, s // tk),
        in_specs=[
            pl.BlockSpec((tq, N_HEADS * HEAD_DIM), lambda i, j: (i, 0)),
            pl.BlockSpec((tk, kvw), lambda i, j: (live_j(i, j), 0)),
            pl.BlockSpec((A_KV_HEADS * V_ROWS, tk), lambda i, j: (0, live_j(i, j))),
            pl.BlockSpec((None, tk, tq), lambda i, j: (i, live_j(i, j), 0)),
        ],
        out_specs=pl.BlockSpec((tq, N_HEADS * HEAD_DIM), lambda i, j: (i, 0)),
        out_shape=jax.ShapeDtypeStruct((s, N_HEADS * HEAD_DIM), MXU_DTYPE),
        scratch_shapes=_flash_scratch(tq, tk) + [pltpu.VMEM((tk, tq), jnp.float32)],
        compiler_params=_cparams(("parallel", "arbitrary")),
        name="dsa_attention",
    )(q, k, v_t, bias)


def _moba_attn_kernel(q_ref, k_ref, vt_ref, km_ref, o_ref, m_ref, al_ref, s_ref, acc_ref, qa_ref, *, n_sel, bps):
    i, j = pl.program_id(0), pl.program_id(1)
    t = MOBA_BLOCK
    tk = bps * t

    @pl.when(j == 0)
    def _():
        _flash_init(m_ref, acc_ref)
        blk = lax.broadcasted_iota(jnp.int32, (LANES, t), 0)
        blk_f = blk.astype(jnp.float32)
        for h in range(N_HEADS):
            sl = slice(h * HEAD_DIM, (h + 1) * HEAD_DIM)
            gate = lax.dot_general(km_ref[:, sl].astype(MXU_DTYPE), q_ref[:, sl], _NT,
                                   preferred_element_type=jnp.float32)
            gate = jnp.where(blk < i, gate, -jnp.inf)
            bias = jnp.full((LANES, t), NEG, jnp.float32)
            for _ in range(n_sel):
                top = jnp.max(gate, axis=0, keepdims=True)
                first = jnp.min(jnp.where(gate == top, blk_f, float(LANES)), axis=0, keepdims=True)
                hit = blk_f == first
                bias = jnp.where(hit, 0.0, bias)
                gate = jnp.where(hit, -jnp.inf, gate)
            bias = jnp.where(blk < i, bias, NEG)
            qa_ref[h, :, 0:HEAD_DIM] = q_ref[:, sl]
            qa_ref[h, :, HEAD_DIM:HEAD_DIM + LANES] = bias.T.astype(qa_ref.dtype)

    def step(on_diagonal):
        shift = MOBA_BLOCK.bit_length() - 1
        key_blk = j * bps + lax.shift_right_logical(lax.broadcasted_iota(jnp.int32, (tk, LANES), 0), shift)
        slot = lax.broadcasted_iota(jnp.int32, (tk, LANES), 1)
        onehot = key_blk == slot
        if on_diagonal:
            onehot = jnp.logical_and(onehot, key_blk != i)
            kpos = j * tk + lax.broadcasted_iota(jnp.int32, (tk, t), 0)
            qpos = i * t + lax.broadcasted_iota(jnp.int32, (tk, t), 1)
            own_future = jnp.logical_and(lax.shift_right_logical(kpos, shift) == i, kpos > qpos)
            causal = jnp.where(own_future, NEG, 0.0)
        e = jnp.where(onehot, 1.0, 0.0).astype(MXU_DTYPE)

        def scores(h):
            k_aug = jnp.concatenate([k_ref[:, h * HEAD_DIM:(h + 1) * HEAD_DIM], e], axis=1)
            s_t = lax.dot_general(k_aug, qa_ref[h], _NT, preferred_element_type=jnp.float32)
            return s_t + causal if on_diagonal else s_t

        def values(h):
            return vt_ref[h * V_ROWS:(h + 1) * V_ROWS, :]

        _flash_tile(scores, values, m_ref, al_ref, s_ref, acc_ref)

    @pl.when((j + 1) * bps <= i)
    def _():
        step(False)

    @pl.when(jnp.logical_and(j * bps <= i, (j + 1) * bps > i))
    def _():
        step(True)

    @pl.when(j == pl.num_programs(1) - 1)
    def _():
        _flash_finish(o_ref, acc_ref)


def moba_attention(q, k, v_t, kmean_pad, n_sel, bps=4):
    s, hd = q.shape
    t = MOBA_BLOCK
    nb = s // t
    bps = min(bps, nb)
    tk = bps * t

    def live_j(i, j):
        return jnp.minimum(j, i // bps)

    return pl.pallas_call(
        functools.partial(_moba_attn_kernel, n_sel=n_sel, bps=bps),
        grid=(nb, nb // bps),
        in_specs=[
            pl.BlockSpec((t, hd), lambda i, j: (i, 0)),
            pl.BlockSpec((tk, hd), lambda i, j: (live_j(i, j), 0)),
            pl.BlockSpec((N_HEADS * V_ROWS, tk), lambda i, j: (0, live_j(i, j))),
            pl.BlockSpec((LANES, hd), lambda i, j: (0, 0)),
        ],
        out_specs=pl.BlockSpec((t, hd), lambda i, j: (i, 0)),
        out_shape=jax.ShapeDtypeStruct((s, hd), MXU_DTYPE),
        scratch_shapes=_flash_scratch(t, tk) + [pltpu.VMEM((N_HEADS, t, HEAD_DIM + LANES), MXU_DTYPE)],
        compiler_params=_cparams(("parallel", "arbitrary")),
        name="moba_attention",
    )(q, k, v_t, kmean_pad)


def _dsa_mixer(x, g, sc, sh, w_in, layer, tabs_main, tabs_idx):
    s = x.shape[0]
    hd = N_HEADS * HEAD_DIM
    kvw = A_KV_HEADS * HEAD_DIM
    n_main = hd + 2 * kvw + IDX_HEADS * IDX_DIM
    n_tail = w_in.shape[2] - n_main
    y = norm_mod_matmul(x, g, sc, sh, w_in, layer, n_main, jnp.float32)
    w_tail = jnp.pad(w_in[layer, :, n_main:], ((0, 0), (0, LANES - n_tail)))[None]
    y_tail = norm_mod_matmul(x, g, sc, sh, w_tail, 0, LANES, jnp.float32, tn=LANES)
    q = rope_cast(y, tabs_main, 0, hd, ROT_DIM // 2, out_scale=HEAD_DIM ** -0.5 * LOG2E)
    k = rope_cast(y, tabs_main, hd, kvw, ROT_DIM // 2)
    v_t = transpose_values(y, hd + kvw, A_KV_HEADS)
    qi = rope_cast(y, tabs_idx, hd + 2 * kvw, IDX_HEADS * IDX_DIM, IDX_ROT_DIM // 2)
    kid, wit = idx_tail(y_tail, tabs_idx)
    n_sel = min(IDX_TOPK_MAX, s // 4)
    bias = dsa_index_bias(qi, kid, wit, n_sel)
    return dsa_attention(q, k, v_t, bias)


def _moba_mixer(x, g, sc, sh, w_in, layer, tabs_main):
    s = x.shape[0]
    hd = N_HEADS * HEAD_DIM
    y = norm_mod_matmul(x, g, sc, sh, w_in, layer, 3 * hd, jnp.float32)
    q = rope_cast(y, tabs_main, 0, hd, ROT_DIM // 2, out_scale=HEAD_DIM ** -0.5 * LOG2E)
    k, kmean = rope_cast_kmean(y, tabs_main, hd, hd, ROT_DIM // 2)
    v_t = transpose_values(y, 2 * hd, N_HEADS)
    nb = s // MOBA_BLOCK
    kmean_pad = jnp.pad(kmean, ((0, LANES - nb), (0, 0)))
    return moba_attention(q, k, v_t, kmean_pad, min(MOBA_TOPK, nb))


def kernel(x, c, positions, a_w_in, a_w_o, b_w_in, b_w_o, ada_w, ada_b, norm_g, ffn_w_in, ffn_w_out):
    b, s, d = x.shape
    assert b == 1 and s % MOBA_BLOCK == 0 and s // MOBA_BLOCK <= LANES
    depth = ada_w.shape[0]
    xs = x[0]
    mod = ada_modulation(c, ada_w, ada_b)
    pos = positions[0]
    tabs_main = rope_tables(pos, ROT_DIM, HEAD_DIM)
    tabs_idx = rope_tables(pos, IDX_ROT_DIM, IDX_DIM)
    for i in range(depth):
        sh1, sc1, g1, sh2, sc2, g2 = [mod[i, :, t * d:(t + 1) * d] for t in range(6)]
        ng = [norm_g[i, t][None, :] for t in range(4)]
        if i % 2 == 0:
            o = _dsa_mixer(xs, ng[0], sc1, sh1, a_w_in, i // 2, tabs_main, tabs_idx)
            w_o = a_w_o
        else:
            o = _moba_mixer(xs, ng[0], sc1, sh1, b_w_in, i // 2, tabs_main)
            w_o = b_w_o
        xs = matmul_postnorm_residual(o, w_o, i // 2, xs, ng[1], g1)
        act = norm_mod_swiglu(xs, ng[2], sc2, sh2, ffn_w_in, i)
        xs = matmul_postnorm_residual(act, ffn_w_out, i, xs, ng[3], g2)
    return xs[None]
```

```python
import functools
import math

import jax
import jax.numpy as jnp
from jax import lax
from jax.experimental import pallas as pl
from jax.experimental.pallas import tpu as pltpu

N_HEADS = 16
HEAD_DIM = 128
ROT_DIM = HEAD_DIM // 4
ROPE_THETA = 500000.0
EPS = 1e-6
A_KV_HEADS = 4
IDX_HEADS = 16
IDX_DIM = 64
IDX_ROT_DIM = IDX_DIM // 4
IDX_TOPK_MAX = 256
MOBA_BLOCK = 256
MOBA_TOPK = 3

LANES = 128
SUBLANES = 8
VMEM_LIMIT_BYTES = 56 * 1024 * 1024

MXU_DTYPE = jnp.bfloat16
BF16_SUBLANES = 16
V_ROWS = HEAD_DIM + BF16_SUBLANES
NEG = -1e30
LOG2E = math.log2(math.e)

_NT = (((1,), (1,)), ((), ()))


def _cparams(sem):
    return pltpu.CompilerParams(dimension_semantics=sem, vmem_limit_bytes=VMEM_LIMIT_BYTES)


def _ada_kernel(ct_ref, w_ref, b_ref, o_ref):
    ct = ct_ref[...]
    ca = ct * (1.0 / (1.0 + jnp.exp(-ct)))
    o_ref[...] = jnp.sum(ca * w_ref[...], axis=0, keepdims=True) + b_ref[...]


def ada_modulation(c, ada_w, ada_b, tn=1024):
    depth, d, n = ada_w.shape
    ct = c.reshape(d, 1)
    return pl.pallas_call(
        _ada_kernel,
        grid=(depth, n // tn),
        in_specs=[
            pl.BlockSpec((d, 1), lambda l, j: (0, 0)),
            pl.BlockSpec((None, d, tn), lambda l, j: (l, 0, j)),
            pl.BlockSpec((None, 1, tn), lambda l, j: (l, 0, j)),
        ],
        out_specs=pl.BlockSpec((None, 1, tn), lambda l, j: (l, 0, j)),
        out_shape=jax.ShapeDtypeStruct((depth, 1, n), jnp.float32),
        compiler_params=_cparams(("parallel", "parallel")),
        name="ada_modulation",
    )(ct, ada_w, ada_b.reshape(depth, 1, n))


def _norm_mod(x, g, sc, sh):
    ms = jnp.mean(x * x, axis=-1, keepdims=True)
    y = x * lax.rsqrt(ms + EPS)
    return (y * g) * (1.0 + sc) + sh


def _nm_swiglu_kernel(x_ref, g_ref, sc_ref, sh_ref, wu_ref, wg_ref, o_ref, h_ref):
    @pl.when(pl.program_id(1) == 0)
    def _():
        h_ref[...] = _norm_mod(x_ref[...], g_ref[...], sc_ref[...], sh_ref[...]).astype(h_ref.dtype)

    h = h_ref[...]
    u = jnp.dot(h, wu_ref[...].astype(MXU_DTYPE), preferred_element_type=jnp.float32)
    gt = jnp.dot(h, wg_ref[...].astype(MXU_DTYPE), preferred_element_type=jnp.float32)
    o_ref[...] = ((gt * (1.0 / (1.0 + jnp.exp(-gt)))) * u).astype(o_ref.dtype)


def norm_mod_swiglu(x, g, sc, sh, w_in, layer, tm=1024, tn=512):
    s, d = x.shape
    f = w_in.shape[2] // 2
    nj = f // tn
    return pl.pallas_call(
        _nm_swiglu_kernel,
        grid=(s // tm, nj),
        in_specs=[
            pl.BlockSpec((tm, d), lambda i, j: (i, 0)),
            pl.BlockSpec((1, d), lambda i, j: (0, 0)),
            pl.BlockSpec((1, d), lambda i, j: (0, 0)),
            pl.BlockSpec((1, d), lambda i, j: (0, 0)),
            pl.BlockSpec((None, d, tn), lambda i, j: (layer, 0, j)),
            pl.BlockSpec((None, d, tn), lambda i, j: (layer, 0, j + nj)),
        ],
        out_specs=pl.BlockSpec((tm, tn), lambda i, j: (i, j)),
        out_shape=jax.ShapeDtypeStruct((s, f), MXU_DTYPE),
        scratch_shapes=[pltpu.VMEM((tm, d), MXU_DTYPE)],
        compiler_params=_cparams(("parallel", "arbitrary")),
        name="norm_mod_swiglu",
    )(x, g, sc, sh, w_in, w_in)


def _mm_postnorm_kernel(a_ref, w_ref, x_ref, g_ref, gate_ref, o_ref):
    k = pl.program_id(1)

    @pl.when(k == 0)
    def _():
        o_ref[...] = jnp.zeros_like(o_ref)

    o_ref[...] += jnp.dot(a_ref[...], w_ref[...].astype(MXU_DTYPE), preferred_element_type=jnp.float32)

    @pl.when(k == pl.num_programs(1) - 1)
    def _():
        y = o_ref[...]
        ms = jnp.mean(y * y, axis=-1, keepdims=True)
        yn = (y * lax.rsqrt(ms + EPS)) * g_ref[...]
        o_ref[...] = x_ref[...] + gate_ref[...] * yn


def matmul_postnorm_residual(a, w, layer, x, g, gate, tm=1024, tk=512):
    s, kdim = a.shape
    d = w.shape[2]
    return pl.pallas_call(
        _mm_postnorm_kernel,
        grid=(s // tm, kdim // tk),
        in_specs=[
            pl.BlockSpec((tm, tk), lambda i, k: (i, k)),
            pl.BlockSpec((None, tk, d), lambda i, k: (layer, k, 0)),
            pl.BlockSpec((tm, d), lambda i, k: (i, 0), pipeline_mode=pl.Buffered(1)),
            pl.BlockSpec((1, d), lambda i, k: (0, 0)),
            pl.BlockSpec((1, d), lambda i, k: (0, 0)),
        ],
        out_specs=pl.BlockSpec((tm, d), lambda i, k: (i, 0)),
        out_shape=jax.ShapeDtypeStruct((s, d), jnp.float32),
        compiler_params=_cparams(("parallel", "arbitrary")),
        name="matmul_postnorm_residual",
    )(a, w, x, g, gate)


def rope_tables(positions, rot_dim, period):
    half = rot_dim // 2
    inv_freq = ROPE_THETA ** (-jnp.arange(half, dtype=jnp.float32) / half)
    ang = positions.astype(jnp.float32)[:, None] * inv_freq
    cos, sin = jnp.cos(ang), jnp.sin(ang)
    s = positions.shape[0]
    one = jnp.ones((s, period - rot_dim), jnp.float32)
    zero = jnp.zeros((s, period - rot_dim), jnp.float32)
    zh = jnp.zeros((s, half), jnp.float32)
    reps = LANES // period
    c = jnp.tile(jnp.concatenate([cos, cos, one], axis=1), (1, reps))
    sa = jnp.tile(jnp.concatenate([-sin, zh, zero], axis=1), (1, reps))
    sb = jnp.tile(jnp.concatenate([zh, sin, zero], axis=1), (1, reps))
    return c, sa, sb


def _rope_lanes(x, c, sa, sb, half):
    return x * c + pltpu.roll(x, LANES - half, 1) * sa + pltpu.roll(x, half, 1) * sb


def _in_proj_kernel(*refs, regions, n_main, has_tail, tm, tn):
    it = iter(refs)
    x_ref, g_ref, sc_ref, sh_ref, w_ref = (next(it) for _ in range(5))
    wt_ref = next(it) if has_tail else None
    tab_main = [next(it) for _ in range(3)]
    tab_idx = [next(it) for _ in range(3)] if has_tail else None
    outs = []
    for kind, _, _ in regions:
        outs.append([next(it), next(it)] if kind == "kmean" else [next(it)])
    tail_outs = [next(it), next(it)] if has_tail else None
    h_ref, acc_ref = next(it), next(it)

    j = pl.program_id(1)
    groups = tn // LANES

    @pl.when(j == 0)
    def _():
        h_ref[...] = _norm_mod(x_ref[...], g_ref[...], sc_ref[...], sh_ref[...]).astype(h_ref.dtype)

    @pl.when(j < n_main)
    def _():
        acc_ref[...] = jnp.dot(h_ref[...], w_ref[...].astype(MXU_DTYPE), preferred_element_type=jnp.float32)

    def rope(gi, tabs, half):
        return _rope_lanes(acc_ref[:, gi * LANES:(gi + 1) * LANES], tabs[0][...], tabs[1][...], tabs[2][...], half)

    def epilogue(kind, out):
        for gi in range(groups):
            sl = slice(gi * LANES, (gi + 1) * LANES)
            if kind == "q":
                out[0][:, sl] = (rope(gi, tab_main, ROT_DIM // 2) * (HEAD_DIM ** -0.5 * LOG2E)).astype(out[0].dtype)
            elif kind == "k":
                out[0][:, sl] = rope(gi, tab_main, ROT_DIM // 2).astype(out[0].dtype)
            elif kind == "kmean":
                r = rope(gi, tab_main, ROT_DIM // 2)
                out[0][:, sl] = r.astype(out[0].dtype)
                for b in range(tm // MOBA_BLOCK):
                    out[1][b:b + 1, sl] = jnp.mean(r[b * MOBA_BLOCK:(b + 1) * MOBA_BLOCK, :], axis=0, keepdims=True)
                out[1][tm // MOBA_BLOCK:, sl] = jnp.zeros((out[1].shape[0] - tm // MOBA_BLOCK, LANES), jnp.float32)
            elif kind == "v":
                r0 = gi * V_ROWS
                out[0][r0:r0 + HEAD_DIM, :] = acc_ref[:, sl].T.astype(out[0].dtype)
                out[0][r0 + HEAD_DIM:r0 + V_ROWS, :] = jnp.ones((V_ROWS - HEAD_DIM, tm), out[0].dtype)
            elif kind == "qidx":
                out[0][:, sl] = rope(gi, tab_idx, IDX_ROT_DIM // 2).astype(out[0].dtype)

    for (kind, start, count), out in zip(regions, outs):
        @pl.when(jnp.logical_and(j >= start, j < start + count))
        def _(kind=kind, out=out):
            epilogue(kind, out)

    if has_tail:
        @pl.when(j == n_main)
        def _():
            y = jnp.dot(h_ref[...], wt_ref[...].astype(MXU_DTYPE), preferred_element_type=jnp.float32)
            r = _rope_lanes(y, tab_idx[0][...], tab_idx[1][...], tab_idx[2][...], IDX_ROT_DIM // 2)
            lane = lax.broadcasted_iota(jnp.int32, y.shape, 1)
            kid_ref, wit_ref = tail_outs
            kid_ref[...] = jnp.where(lane < IDX_DIM, r, pltpu.roll(r, IDX_DIM, 1)).astype(kid_ref.dtype)
            wit_ref[...] = (y * (IDX_HEADS ** -0.5 * IDX_DIM ** -0.5)).T


def mixer_in_proj(x, g, sc, sh, w, layer, regions, tabs_main, tabs_idx=None, w_tail=None, tm=1024, tn=512):
    s, d = x.shape
    has_tail = w_tail is not None
    tiles, start = [], 0
    for kind, n_cols in regions:
        assert n_cols % tn == 0
        tiles.append((kind, start, n_cols // tn))
        start += n_cols // tn
    n_main = start
    heads_per_tile = tn // HEAD_DIM

    def clip(j, first, count):
        return jnp.clip(j - first, 0, count - 1)

    row = pl.BlockSpec((1, d), lambda i, j: (0, 0))
    tab = pl.BlockSpec((tm, LANES), lambda i, j: (i, 0))
    in_specs = [pl.BlockSpec((tm, d), lambda i, j: (i, 0)), row, row, row,
                pl.BlockSpec((None, d, tn), lambda i, j: (layer, 0, jnp.minimum(j, n_main - 1)))]
    args = [x, g, sc, sh, w]
    if has_tail:
        in_specs.append(pl.BlockSpec((d, LANES), lambda i, j: (0, 0)))
        args.append(w_tail)
    in_specs += [tab] * 3
    args += list(tabs_main)
    if has_tail:
        in_specs += [tab] * 3
        args += list(tabs_idx)

    out_specs, out_shape = [], []
    for kind, first, count in tiles:
        if kind == "v":
            out_specs.append(pl.BlockSpec((heads_per_tile * V_ROWS, tm),
                                          lambda i, j, first=first, count=count: (clip(j, first, count), i)))
            out_shape.append(jax.ShapeDtypeStruct((count * heads_per_tile * V_ROWS, s), MXU_DTYPE))
            continue
        out_specs.append(pl.BlockSpec((tm, tn), lambda i, j, first=first, count=count: (i, clip(j, first, count))))
        out_shape.append(jax.ShapeDtypeStruct((s, count * tn), MXU_DTYPE))
        if kind == "kmean":
            out_specs.append(pl.BlockSpec((None, SUBLANES, tn),
                                          lambda i, j, first=first, count=count: (i, 0, clip(j, first, count))))
            out_shape.append(jax.ShapeDtypeStruct((s // tm, SUBLANES, count * tn), jnp.float32))
    if has_tail:
        out_specs += [pl.BlockSpec((tm, LANES), lambda i, j: (i, 0)), pl.BlockSpec((LANES, tm), lambda i, j: (0, i))]
        out_shape += [jax.ShapeDtypeStruct((s, LANES), MXU_DTYPE), jax.ShapeDtypeStruct((LANES, s), jnp.float32)]

    return pl.pallas_call(
        functools.partial(_in_proj_kernel, regions=tiles, n_main=n_main, has_tail=has_tail, tm=tm, tn=tn),
        grid=(s // tm, n_main + (1 if has_tail else 0)),
        in_specs=in_specs,
        out_specs=out_specs,
        out_shape=out_shape,
        scratch_shapes=[pltpu.VMEM((tm, d), MXU_DTYPE), pltpu.VMEM((tm, tn), jnp.float32)],
        compiler_params=_cparams(("parallel", "arbitrary")),
        name="mixer_in_proj",
    )(*args)


def _sortable_key(x):
    b = pltpu.bitcast(x, jnp.int32)
    return jnp.where(b < 0, b ^ jnp.int32(0x7FFFFFFF), b)


I16_MIN_ABS = 2 ** 15


def _indexer_kernel(qi_ref, kid_ref, wit_ref, bias_ref, key_ref, hi_ref, lo_ref, qm_ref, *, tq, tk, n_sel):
    i = pl.program_id(0)
    s = bias_ref.shape[0]
    n_live = (i * tq + tq + tk - 1) // tk
    krow = lax.broadcasted_iota(jnp.int32, (tk, tq), 0)
    qcol = i * tq + lax.broadcasted_iota(jnp.int32, (tk, tq), 1)
    lane = lax.broadcasted_iota(jnp.int32, (tq, LANES), 1)

    for p in range(IDX_HEADS // 2):
        qp = qi_ref[:, p * LANES:(p + 1) * LANES]
        zero = jnp.zeros_like(qp)
        qm_ref[2 * p] = jnp.where(lane < IDX_DIM, qp, zero)
        qm_ref[2 * p + 1] = jnp.where(lane >= IDX_DIM, qp, zero)

    def score_tile(c, carry):
        k0 = pl.multiple_of(c * tk, tk)
        kc = kid_ref[pl.ds(k0, tk), :]
        acc = jnp.zeros((tk, tq), jnp.float32)
        for h in range(IDX_HEADS):
            d = lax.dot_general(kc, qm_ref[h], _NT, preferred_element_type=jnp.float32)
            acc = acc + jnp.maximum(d, 0.0) * wit_ref[IDX_DIM + h:IDX_DIM + h + 1, :]
        causal = (k0 + krow) <= qcol
        key = _sortable_key(jnp.where(causal, acc, -jnp.inf))
        key_ref[pl.ds(k0, tk), :] = key
        hi_ref[pl.ds(k0, tk), :] = lax.shift_right_arithmetic(key, 16).astype(jnp.int16)
        lo_ref[pl.ds(k0, tk), :] = ((key & 0xFFFF) - I16_MIN_ABS).astype(jnp.int16)
        return carry

    lax.fori_loop(0, n_live, score_tile, 0)

    one16 = jnp.ones((tk, tq), jnp.int16)
    zero16 = jnp.zeros((tk, tq), jnp.int16)
    rows = 2 * BF16_SUBLANES

    def count_ge(ref, cand):
        cand16 = cand.astype(jnp.int16)

        def body(c, cnt):
            k0 = pl.multiple_of(c * tk, tk)
            ind = jnp.where(ref[pl.ds(k0, tk), :] >= cand16, one16, zero16)
            part = [cnt, jnp.zeros_like(cnt)]
            for r in range(tk // rows):
                part[r % 2] = part[r % 2] + ind[r * rows:(r + 1) * rows, :]
            return part[0] + part[1]
        cnt = lax.fori_loop(0, n_live, body, jnp.zeros((rows, tq), jnp.int16))
        return jnp.sum(cnt.astype(jnp.float32), axis=0, keepdims=True)

    def kth_largest16(ref, need):
        zero_c = jnp.zeros((1, tq), jnp.int32)
        tau = jnp.where(count_ge(ref, zero_c) >= need, zero_c, jnp.full((1, tq), -I16_MIN_ABS, jnp.int32))

        def bit_step(b, tau):
            cand = tau + lax.shift_left(jnp.int32(1), 14 - b)
            return jnp.where(count_ge(ref, cand) >= need, cand, tau)

        return lax.fori_loop(0, 15, bit_step, tau)

    need = jnp.full((1, tq), float(n_sel), jnp.float32)
    t_hi = kth_largest16(hi_ref, need)
    above = jnp.where(t_hi == I16_MIN_ABS - 1, 0.0,
                      count_ge(hi_ref, jnp.minimum(t_hi + 1, I16_MIN_ABS - 1)))
    t_hi16 = t_hi.astype(jnp.int16)

    def keep_ties(c, carry):
        k0 = pl.multiple_of(c * tk, tk)
        tie = hi_ref[pl.ds(k0, tk), :] == t_hi16
        lo_ref[pl.ds(k0, tk), :] = jnp.where(tie, lo_ref[pl.ds(k0, tk), :], jnp.full((tk, tq), -I16_MIN_ABS, jnp.int16))
        return carry

    lax.fori_loop(0, n_live, keep_ties, 0)
    t_lo = kth_largest16(lo_ref, need - above)
    tau = lax.shift_left(t_hi, 16) | ((t_lo + I16_MIN_ABS) & 0xFFFF)

    def write_live(c, carry):
        k0 = pl.multiple_of(c * tk, tk)
        causal = (k0 + krow) <= qcol
        sel = jnp.where(key_ref[pl.ds(k0, tk), :] >= tau, 0.0, NEG)
        bias_ref[pl.ds(k0, tk), :] = jnp.where(causal, sel, NEG).astype(bias_ref.dtype)
        return carry

    lax.fori_loop(0, n_live, write_live, 0)

    def write_dead(c, carry):
        k0 = pl.multiple_of(c * tk, tk)
        bias_ref[pl.ds(k0, tk), :] = jnp.full((tk, tq), NEG, bias_ref.dtype)
        return carry

    lax.fori_loop(n_live, s // tk, write_dead, 0)


def dsa_index_bias(qi, kid, wit, n_sel, tq=256, tk=512):
    s = qi.shape[0]
    return pl.pallas_call(
        functools.partial(_indexer_kernel, tq=tq, tk=tk, n_sel=n_sel),
        grid=(s // tq,),
        in_specs=[
            pl.BlockSpec((tq, IDX_HEADS * IDX_DIM), lambda i: (i, 0)),
            pl.BlockSpec((s, LANES), lambda i: (0, 0)),
            pl.BlockSpec((LANES, tq), lambda i: (0, i)),
        ],
        out_specs=pl.BlockSpec((None, s, tq), lambda i: (i, 0, 0)),
        out_shape=jax.ShapeDtypeStruct((s // tq, s, tq), MXU_DTYPE),
        scratch_shapes=[pltpu.VMEM((s, tq), jnp.int32),
                        pltpu.VMEM((s, tq), jnp.int16),
                        pltpu.VMEM((s, tq), jnp.int16),
                        pltpu.VMEM((IDX_HEADS, tq, LANES), MXU_DTYPE)],
        compiler_params=_cparams(("parallel",)),
        name="dsa_index_bias",
    )(qi, kid, wit)


def _flash_stage_a(h, s_t, m_ref, al_ref, s_ref):
    m_prev = m_ref[h]
    m_new = jnp.maximum(m_prev, jnp.max(s_t, axis=0, keepdims=True))
    al_ref[h] = jnp.exp2(m_prev - m_new)
    m_ref[h] = m_new
    s_ref[h % 2] = s_t


def _flash_stage_b(h, v_aug, m_ref, al_ref, s_ref, acc_ref):
    p_t = jnp.exp2(s_ref[h % 2] - m_ref[h]).astype(MXU_DTYPE)
    acc_ref[h] = al_ref[h] * acc_ref[h] + jnp.dot(v_aug, p_t, preferred_element_type=jnp.float32)


def _flash_tile(scores, values, m_ref, al_ref, s_ref, acc_ref):
    _flash_stage_a(0, scores(0), m_ref, al_ref, s_ref)
    for h in range(N_HEADS):
        if h + 1 < N_HEADS:
            _flash_stage_a(h + 1, scores(h + 1), m_ref, al_ref, s_ref)
        _flash_stage_b(h, values(h), m_ref, al_ref, s_ref, acc_ref)


def _flash_init(m_ref, acc_ref):
    m_ref[...] = jnp.full(m_ref.shape, NEG, jnp.float32)
    acc_ref[...] = jnp.zeros_like(acc_ref)


def _flash_finish(o_ref, acc_ref):
    for h in range(N_HEADS):
        out_t = acc_ref[h, 0:HEAD_DIM, :] / acc_ref[h, HEAD_DIM:HEAD_DIM + 1, :]
        o_ref[:, h * HEAD_DIM:(h + 1) * HEAD_DIM] = out_t.T.astype(o_ref.dtype)


def _flash_scratch(tq, tk):
    return [pltpu.VMEM((N_HEADS, 1, tq), jnp.float32),
            pltpu.VMEM((N_HEADS, 1, tq), jnp.float32),
            pltpu.VMEM((2, tk, tq), jnp.float32),
            pltpu.VMEM((N_HEADS, V_ROWS, tq), jnp.float32)]


def _dsa_attn_kernel(q_ref, k_ref, vt_ref, bias_ref, o_ref, m_ref, al_ref, s_ref, acc_ref, bias_f32_ref,
                     *, tq, tk):
    i, j = pl.program_id(0), pl.program_id(1)
    rep = N_HEADS // A_KV_HEADS

    @pl.when(j == 0)
    def _():
        _flash_init(m_ref, acc_ref)

    @pl.when(j * tk < (i + 1) * tq)
    def _():
        bias_f32_ref[...] = bias_ref[...].astype(jnp.float32)

        def scores(h):
            g = h // rep
            kg = k_ref[:, g * HEAD_DIM:(g + 1) * HEAD_DIM]
            qh = q_ref[:, h * HEAD_DIM:(h + 1) * HEAD_DIM]
            return lax.dot_general(kg, qh, _NT, preferred_element_type=jnp.float32) + bias_f32_ref[...]

        def values(h):
            g = h // rep
            return vt_ref[g * V_ROWS:(g + 1) * V_ROWS, :]

        _flash_tile(scores, values, m_ref, al_ref, s_ref, acc_ref)

    @pl.when(j == pl.num_programs(1) - 1)
    def _():
        _flash_finish(o_ref, acc_ref)


def dsa_attention(q, k, v_t, bias, tq=256, tk=1024):
    s = q.shape[0]
    tk = min(tk, s)
    kvw = A_KV_HEADS * HEAD_DIM

    def live_j(i, j):
        return jnp.minimum(j, ((i + 1) * tq - 1) // tk)

    return pl.pallas_call(
        functools.partial(_dsa_attn_kernel, tq=tq, tk=tk),
        grid=(s // tq, s // tk),
        in_specs=[
            pl.BlockSpec((tq, N_HEADS * HEAD_DIM), lambda i, j: (i, 0)),
            pl.BlockSpec((tk, kvw), lambda i, j: (live_j(i, j), 0)),
            pl.BlockSpec((A_KV_HEADS * V_ROWS, tk), lambda i, j: (0, live_j(i, j))),
            pl.BlockSpec((None, tk, tq), lambda i, j: (i, live_j(i, j), 0)),
        ],
        out_specs=pl.BlockSpec((tq, N_HEADS * HEAD_DIM), lambda i, j: (i, 0)),
        out_shape=jax.ShapeDtypeStruct((s, N_HEADS * HEAD_DIM), MXU_DTYPE),
        scratch_shapes=_flash_scratch(tq, tk) + [pltpu.VMEM((tk, tq), jnp.float32)],
        compiler_params=_cparams(("parallel", "arbitrary")),
        name="dsa_attention",
    )(q, k, v_t, bias)


def _moba_attn_kernel(q_ref, k_ref, vt_ref, km_ref, o_ref, m_ref, al_ref, s_ref, acc_ref, qa_ref, *, n_sel, bps):
    i, j = pl.program_id(0), pl.program_id(1)
    t = MOBA_BLOCK
    tk = bps * t

    @pl.when(j == 0)
    def _():
        _flash_init(m_ref, acc_ref)
        blk = lax.broadcasted_iota(jnp.int32, (LANES, t), 0)
        blk_f = blk.astype(jnp.float32)
        for h in range(N_HEADS):
            sl = slice(h * HEAD_DIM, (h + 1) * HEAD_DIM)
            gate = lax.dot_general(km_ref[:, sl].astype(MXU_DTYPE), q_ref[:, sl], _NT,
                                   preferred_element_type=jnp.float32)
            gate = jnp.where(blk < i, gate, -jnp.inf)
            bias = jnp.full((LANES, t), NEG, jnp.float32)
            for _ in range(n_sel):
                top = jnp.max(gate, axis=0, keepdims=True)
                first = jnp.min(jnp.where(gate == top, blk_f, float(LANES)), axis=0, keepdims=True)
                hit = blk_f == first
                bias = jnp.where(hit, 0.0, bias)
                gate = jnp.where(hit, -jnp.inf, gate)
            bias = jnp.where(blk < i, bias, NEG)
            qa_ref[h, :, 0:HEAD_DIM] = q_ref[:, sl]
            qa_ref[h, :, HEAD_DIM:HEAD_DIM + LANES] = bias.T.astype(qa_ref.dtype)

    def step(on_diagonal):
        shift = MOBA_BLOCK.bit_length() - 1
        key_blk = j * bps + lax.shift_right_logical(lax.broadcasted_iota(jnp.int32, (tk, LANES), 0), shift)
        slot = lax.broadcasted_iota(jnp.int32, (tk, LANES), 1)
        onehot = key_blk == slot
        if on_diagonal:
            onehot = jnp.logical_and(onehot, key_blk != i)
            kpos = j * tk + lax.broadcasted_iota(jnp.int32, (tk, t), 0)
            qpos = i * t + lax.broadcasted_iota(jnp.int32, (tk, t), 1)
            own_future = jnp.logical_and(lax.shift_right_logical(kpos, shift) == i, kpos > qpos)
            causal = jnp.where(own_future, NEG, 0.0)
        e = jnp.where(onehot, 1.0, 0.0).astype(MXU_DTYPE)

        def scores(h):
            k_aug = jnp.concatenate([k_ref[:, h * HEAD_DIM:(h + 1) * HEAD_DIM], e], axis=1)
            s_t = lax.dot_general(k_aug, qa_ref[h], _NT, preferred_element_type=jnp.float32)
            return s_t + causal if on_diagonal else s_t

        def values(h):
            return vt_ref[h * V_ROWS:(h + 1) * V_ROWS, :]

        _flash_tile(scores, values, m_ref, al_ref, s_ref, acc_ref)

    @pl.when((j + 1) * bps <= i)
    def _():
        step(False)

    @pl.when(jnp.logical_and(j * bps <= i, (j + 1) * bps > i))
    def _():
        step(True)

    @pl.when(j == pl.num_programs(1) - 1)
    def _():
        _flash_finish(o_ref, acc_ref)


def moba_attention(q, k, v_t, kmean_pad, n_sel, bps=4):
    s, hd = q.shape
    t = MOBA_BLOCK
    nb = s // t
    bps = min(bps, nb)
    tk = bps * t

    def live_j(i, j):
        return jnp.minimum(j, i // bps)

    return pl.pallas_call(
        functools.partial(_moba_attn_kernel, n_sel=n_sel, bps=bps),
        grid=(nb, nb // bps),
        in_specs=[
            pl.BlockSpec((t, hd), lambda i, j: (i, 0)),
            pl.BlockSpec((tk, hd), lambda i, j: (live_j(i, j), 0)),
            pl.BlockSpec((N_HEADS * V_ROWS, tk), lambda i, j: (0, live_j(i, j))),
            pl.BlockSpec((LANES, hd), lambda i, j: (0, 0)),
        ],
        out_specs=pl.BlockSpec((t, hd), lambda i, j: (i, 0)),
        out_shape=jax.ShapeDtypeStruct((s, hd), MXU_DTYPE),
        scratch_shapes=_flash_scratch(t, tk) + [pltpu.VMEM((N_HEADS, t, HEAD_DIM + LANES), MXU_DTYPE)],
        compiler_params=_cparams(("parallel", "arbitrary")),
        name="moba_attention",
    )(q, k, v_t, kmean_pad)


def _dsa_mixer(x, g, sc, sh, w_in, layer, tabs_main, tabs_idx):
    s = x.shape[0]
    hd = N_HEADS * HEAD_DIM
    kvw = A_KV_HEADS * HEAD_DIM
    regions = [("q", hd), ("k", kvw), ("v", kvw), ("qidx", IDX_HEADS * IDX_DIM)]
    n_main = sum(n for _, n in regions)
    n_tail = w_in.shape[2] - n_main
    w_tail = jnp.pad(w_in[layer, :, n_main:], ((0, 0), (0, LANES - n_tail)))
    q, k, v_t, qi, kid, wit = mixer_in_proj(x, g, sc, sh, w_in, layer, regions, tabs_main, tabs_idx, w_tail)
    n_sel = min(IDX_TOPK_MAX, s // 4)
    bias = dsa_index_bias(qi, kid, wit, n_sel)
    return dsa_attention(q, k, v_t, bias)


def _moba_mixer(x, g, sc, sh, w_in, layer, tabs_main):
    s = x.shape[0]
    hd = N_HEADS * HEAD_DIM
    tm = 1024
    q, k, kmean, v_t = mixer_in_proj(x, g, sc, sh, w_in, layer, [("q", hd), ("kmean", hd), ("v", hd)],
                                     tabs_main, tm=tm)
    nb = s // MOBA_BLOCK
    kmean = kmean[:, :tm // MOBA_BLOCK, :].reshape(nb, hd)
    kmean_pad = jnp.pad(kmean, ((0, LANES - nb), (0, 0)))
    return moba_attention(q, k, v_t, kmean_pad, min(MOBA_TOPK, nb))


def kernel(x, c, positions, a_w_in, a_w_o, b_w_in, b_w_o, ada_w, ada_b, norm_g, ffn_w_in, ffn_w_out):
    b, s, d = x.shape
    assert b == 1 and s % MOBA_BLOCK == 0 and s // MOBA_BLOCK <= LANES
    depth = ada_w.shape[0]
    xs = x[0]
    mod = ada_modulation(c, ada_w, ada_b)
    pos = positions[0]
    tabs_main = rope_tables(pos, ROT_DIM, HEAD_DIM)
    tabs_idx = rope_tables(pos, IDX_ROT_DIM, IDX_DIM)
    for i in range(depth):
        sh1, sc1, g1, sh2, sc2, g2 = [mod[i, :, t * d:(t + 1) * d] for t in range(6)]
        ng = [norm_g[i, t][None, :] for t in range(4)]
        if i % 2 == 0:
            o = _dsa_mixer(xs, ng[0], sc1, sh1, a_w_in, i // 2, tabs_main, tabs_idx)
            w_o = a_w_o
        else:
            o = _moba_mixer(xs, ng[0], sc1, sh1, b_w_in, i // 2, tabs_main)
            w_o = b_w_o
        xs = matmul_postnorm_residual(o, w_o, i // 2, xs, ng[1], g1)
        act = norm_mod_swiglu(xs, ng[2], sc2, sh2, ffn_w_in, i)
        xs = matmul_postnorm_residual(act, ffn_w_out, i, xs, ng[3], g2)
    return xs[None]
```

```python
import functools
import math

import jax
import jax.numpy as jnp
from jax import lax
from jax.experimental import pallas as pl
from jax.experimental.pallas import tpu as pltpu

N_HEADS = 16
HEAD_DIM = 128
ROT_DIM = HEAD_DIM // 4
ROPE_THETA = 500000.0
EPS = 1e-6
A_KV_HEADS = 4
IDX_HEADS = 16
IDX_DIM = 64
IDX_ROT_DIM = IDX_DIM // 4
IDX_TOPK_MAX = 256
MOBA_BLOCK = 256
MOBA_TOPK = 3

LANES = 128
SUBLANES = 8
VMEM_LIMIT_BYTES = 56 * 1024 * 1024

MXU_DTYPE = jnp.bfloat16
BF16_SUBLANES = 16
V_ROWS = HEAD_DIM + BF16_SUBLANES
NEG = -1e30
LOG2E = math.log2(math.e)

_NT = (((1,), (1,)), ((), ()))


def _cparams(sem):
    return pltpu.CompilerParams(dimension_semantics=sem, vmem_limit_bytes=VMEM_LIMIT_BYTES)


def _ada_kernel(ct_ref, w_ref, b_ref, o_ref):
    ct = ct_ref[...]
    ca = ct * (1.0 / (1.0 + jnp.exp(-ct)))
    o_ref[...] = jnp.sum(ca * w_ref[...], axis=0, keepdims=True) + b_ref[...]


def ada_modulation(c, ada_w, ada_b, tn=1024):
    depth, d, n = ada_w.shape
    ct = c.reshape(d, 1)
    return pl.pallas_call(
        _ada_kernel,
        grid=(depth, n // tn),
        in_specs=[
            pl.BlockSpec((d, 1), lambda l, j: (0, 0)),
            pl.BlockSpec((None, d, tn), lambda l, j: (l, 0, j)),
            pl.BlockSpec((None, 1, tn), lambda l, j: (l, 0, j)),
        ],
        out_specs=pl.BlockSpec((None, 1, tn), lambda l, j: (l, 0, j)),
        out_shape=jax.ShapeDtypeStruct((depth, 1, n), jnp.float32),
        compiler_params=_cparams(("parallel", "parallel")),
        name="ada_modulation",
    )(ct, ada_w, ada_b.reshape(depth, 1, n))


def _norm_mod(x, g, sc, sh):
    ms = jnp.mean(x * x, axis=-1, keepdims=True)
    y = x * lax.rsqrt(ms + EPS)
    return (y * g) * (1.0 + sc) + sh


def _mlp_kernel(x_ref, g_ref, sc_ref, sh_ref, wu_ref, wg_ref, wo_ref, gp_ref, gate_ref, o_ref, h_ref):
    j = pl.program_id(1)

    @pl.when(j == 0)
    def _():
        h_ref[...] = _norm_mod(x_ref[...], g_ref[...], sc_ref[...], sh_ref[...]).astype(h_ref.dtype)
        o_ref[...] = jnp.zeros_like(o_ref)

    h = h_ref[...]
    u = jnp.dot(h, wu_ref[...].astype(MXU_DTYPE), preferred_element_type=jnp.float32)
    gt = jnp.dot(h, wg_ref[...].astype(MXU_DTYPE), preferred_element_type=jnp.float32)
    act = ((gt * (1.0 / (1.0 + jnp.exp(-gt)))) * u).astype(MXU_DTYPE)
    o_ref[...] += jnp.dot(act, wo_ref[...].astype(MXU_DTYPE), preferred_element_type=jnp.float32)

    @pl.when(j == pl.num_programs(1) - 1)
    def _():
        y = o_ref[...]
        ms = jnp.mean(y * y, axis=-1, keepdims=True)
        yn = (y * lax.rsqrt(ms + EPS)) * gp_ref[...]
        o_ref[...] = x_ref[...] + gate_ref[...] * yn


def mlp_sublayer(x, g, sc, sh, w_in, w_out, layer, g_post, gate, tm=1024, tf=256):
    s, d = x.shape
    f = w_in.shape[2] // 2
    nj = f // tf
    row = pl.BlockSpec((1, d), lambda i, j: (0, 0))
    return pl.pallas_call(
        _mlp_kernel,
        grid=(s // tm, nj),
        in_specs=[
            pl.BlockSpec((tm, d), lambda i, j: (i, 0), pipeline_mode=pl.Buffered(1)),
            row, row, row,
            pl.BlockSpec((None, d, tf), lambda i, j: (layer, 0, j)),
            pl.BlockSpec((None, d, tf), lambda i, j: (layer, 0, j + nj)),
            pl.BlockSpec((None, tf, d), lambda i, j: (layer, j, 0)),
            row, row,
        ],
        out_specs=pl.BlockSpec((tm, d), lambda i, j: (i, 0)),
        out_shape=jax.ShapeDtypeStruct((s, d), jnp.float32),
        scratch_shapes=[pltpu.VMEM((tm, d), MXU_DTYPE)],
        compiler_params=_cparams(("parallel", "arbitrary")),
        name="mlp_sublayer",
    )(x, g, sc, sh, w_in, w_in, w_out, g_post, gate)


def _mm_postnorm_kernel(a_ref, w_ref, x_ref, g_ref, gate_ref, o_ref):
    k = pl.program_id(1)

    @pl.when(k == 0)
    def _():
        o_ref[...] = jnp.zeros_like(o_ref)

    o_ref[...] += jnp.dot(a_ref[...], w_ref[...].astype(MXU_DTYPE), preferred_element_type=jnp.float32)

    @pl.when(k == pl.num_programs(1) - 1)
    def _():
        y = o_ref[...]
        ms = jnp.mean(y * y, axis=-1, keepdims=True)
        yn = (y * lax.rsqrt(ms + EPS)) * g_ref[...]
        o_ref[...] = x_ref[...] + gate_ref[...] * yn


def matmul_postnorm_residual(a, w, layer, x, g, gate, tm=1024, tk=512):
    s, kdim = a.shape
    d = w.shape[2]
    return pl.pallas_call(
        _mm_postnorm_kernel,
        grid=(s // tm, kdim // tk),
        in_specs=[
            pl.BlockSpec((tm, tk), lambda i, k: (i, k)),
            pl.BlockSpec((None, tk, d), lambda i, k: (layer, k, 0)),
            pl.BlockSpec((tm, d), lambda i, k: (i, 0), pipeline_mode=pl.Buffered(1)),
            pl.BlockSpec((1, d), lambda i, k: (0, 0)),
            pl.BlockSpec((1, d), lambda i, k: (0, 0)),
        ],
        out_specs=pl.BlockSpec((tm, d), lambda i, k: (i, 0)),
        out_shape=jax.ShapeDtypeStruct((s, d), jnp.float32),
        compiler_params=_cparams(("parallel", "arbitrary")),
        name="matmul_postnorm_residual",
    )(a, w, x, g, gate)


def rope_tables(positions, rot_dim, period):
    half = rot_dim // 2
    inv_freq = ROPE_THETA ** (-jnp.arange(half, dtype=jnp.float32) / half)
    ang = positions.astype(jnp.float32)[:, None] * inv_freq
    cos, sin = jnp.cos(ang), jnp.sin(ang)
    s = positions.shape[0]
    one = jnp.ones((s, period - rot_dim), jnp.float32)
    zero = jnp.zeros((s, period - rot_dim), jnp.float32)
    zh = jnp.zeros((s, half), jnp.float32)
    reps = LANES // period
    c = jnp.tile(jnp.concatenate([cos, cos, one], axis=1), (1, reps))
    sa = jnp.tile(jnp.concatenate([-sin, zh, zero], axis=1), (1, reps))
    sb = jnp.tile(jnp.concatenate([zh, sin, zero], axis=1), (1, reps))
    return c, sa, sb


def _rope_lanes(x, c, sa, sb, half):
    return x * c + pltpu.roll(x, LANES - half, 1) * sa + pltpu.roll(x, half, 1) * sb


def _in_proj_kernel(*refs, regions, n_main, has_tail, tm, tn):
    it = iter(refs)
    x_ref, g_ref, sc_ref, sh_ref, w_ref = (next(it) for _ in range(5))
    wt_ref = next(it) if has_tail else None
    tab_main = [next(it) for _ in range(3)]
    tab_idx = [next(it) for _ in range(3)] if has_tail else None
    outs = []
    for kind, _, _ in regions:
        outs.append([next(it), next(it)] if kind == "kmean" else [next(it)])
    tail_outs = [next(it), next(it)] if has_tail else None
    h_ref, acc_ref = next(it), next(it)

    j = pl.program_id(1)

    @pl.when(j == 0)
    def _():
        h_ref[...] = _norm_mod(x_ref[...], g_ref[...], sc_ref[...], sh_ref[...]).astype(h_ref.dtype)

    def rope(y, tabs, half):
        return _rope_lanes(y, tabs[0][...], tabs[1][...], tabs[2][...], half)

    def epilogue(kind, out, gi, y):
        sl = slice(gi * LANES, (gi + 1) * LANES)
        if kind == "q":
            out[0][:, sl] = (rope(y, tab_main, ROT_DIM // 2) * (HEAD_DIM ** -0.5 * LOG2E)).astype(out[0].dtype)
        elif kind == "k":
            out[0][:, sl] = rope(y, tab_main, ROT_DIM // 2).astype(out[0].dtype)
        elif kind == "kmean":
            r = rope(y, tab_main, ROT_DIM // 2)
            out[0][:, sl] = r.astype(out[0].dtype)
            for b in range(tm // MOBA_BLOCK):
                out[1][b:b + 1, sl] = jnp.mean(r[b * MOBA_BLOCK:(b + 1) * MOBA_BLOCK, :], axis=0, keepdims=True)
            out[1][tm // MOBA_BLOCK:, sl] = jnp.zeros((out[1].shape[0] - tm // MOBA_BLOCK, LANES), jnp.float32)
        elif kind == "v":
            r0 = gi * V_ROWS
            out[0][r0:r0 + HEAD_DIM, :] = y.T.astype(out[0].dtype)
            out[0][r0 + HEAD_DIM:r0 + V_ROWS, :] = jnp.ones((V_ROWS - HEAD_DIM, tm), out[0].dtype)
        elif kind == "qidx":
            out[0][:, sl] = rope(y, tab_idx, IDX_ROT_DIM // 2).astype(out[0].dtype)

    def region_step(kind, out):
        half_cols = tn // 2
        for hf in range(2):
            acc_ref[hf] = jnp.dot(h_ref[...], w_ref[:, hf * half_cols:(hf + 1) * half_cols].astype(MXU_DTYPE),
                                  preferred_element_type=jnp.float32)
        for hf in range(2):
            for g2 in range(half_cols // LANES):
                epilogue(kind, out, hf * (half_cols // LANES) + g2, acc_ref[hf, :, g2 * LANES:(g2 + 1) * LANES])

    for (kind, start, count), out in zip(regions, outs):
        @pl.when(jnp.logical_and(j >= start, j < start + count))
        def _(kind=kind, out=out):
            region_step(kind, out)

    if has_tail:
        @pl.when(j == n_main)
        def _():
            y = jnp.dot(h_ref[...], wt_ref[...].astype(MXU_DTYPE), preferred_element_type=jnp.float32)
            r = _rope_lanes(y, tab_idx[0][...], tab_idx[1][...], tab_idx[2][...], IDX_ROT_DIM // 2)
            lane = lax.broadcasted_iota(jnp.int32, y.shape, 1)
            kid_ref, wit_ref = tail_outs
            kid_ref[...] = jnp.where(lane < IDX_DIM, r, pltpu.roll(r, IDX_DIM, 1)).astype(kid_ref.dtype)
            wit_ref[...] = (y * (IDX_HEADS ** -0.5 * IDX_DIM ** -0.5)).T


def mixer_in_proj(x, g, sc, sh, w, layer, regions, tabs_main, tabs_idx=None, w_tail=None, tm=1024, tn=512):
    s, d = x.shape
    has_tail = w_tail is not None
    tiles, start = [], 0
    for kind, n_cols in regions:
        assert n_cols % tn == 0
        tiles.append((kind, start, n_cols // tn))
        start += n_cols // tn
    n_main = start
    heads_per_tile = tn // HEAD_DIM

    def clip(j, first, count):
        return jnp.clip(j - first, 0, count - 1)

    row = pl.BlockSpec((1, d), lambda i, j: (0, 0))
    tab = pl.BlockSpec((tm, LANES), lambda i, j: (i, 0))
    in_specs = [pl.BlockSpec((tm, d), lambda i, j: (i, 0)), row, row, row,
                pl.BlockSpec((None, d, tn), lambda i, j: (layer, 0, jnp.minimum(j, n_main - 1)))]
    args = [x, g, sc, sh, w]
    if has_tail:
        in_specs.append(pl.BlockSpec((d, LANES), lambda i, j: (0, 0)))
        args.append(w_tail)
    in_specs += [tab] * 3
    args += list(tabs_main)
    if has_tail:
        in_specs += [tab] * 3
        args += list(tabs_idx)

    out_specs, out_shape = [], []
    for kind, first, count in tiles:
        if kind == "v":
            out_specs.append(pl.BlockSpec((heads_per_tile * V_ROWS, tm),
                                          lambda i, j, first=first, count=count: (clip(j, first, count), i)))
            out_shape.append(jax.ShapeDtypeStruct((count * heads_per_tile * V_ROWS, s), MXU_DTYPE))
            continue
        out_specs.append(pl.BlockSpec((tm, tn), lambda i, j, first=first, count=count: (i, clip(j, first, count))))
        out_shape.append(jax.ShapeDtypeStruct((s, count * tn), MXU_DTYPE))
        if kind == "kmean":
            out_specs.append(pl.BlockSpec((None, SUBLANES, tn),
                                          lambda i, j, first=first, count=count: (i, 0, clip(j, first, count))))
            out_shape.append(jax.ShapeDtypeStruct((s // tm, SUBLANES, count * tn), jnp.float32))
    if has_tail:
        out_specs += [pl.BlockSpec((tm, LANES), lambda i, j: (i, 0)), pl.BlockSpec((LANES, tm), lambda i, j: (0, i))]
        out_shape += [jax.ShapeDtypeStruct((s, LANES), MXU_DTYPE), jax.ShapeDtypeStruct((LANES, s), jnp.float32)]

    return pl.pallas_call(
        functools.partial(_in_proj_kernel, regions=tiles, n_main=n_main, has_tail=has_tail, tm=tm, tn=tn),
        grid=(s // tm, n_main + (1 if has_tail else 0)),
        in_specs=in_specs,
        out_specs=out_specs,
        out_shape=out_shape,
        scratch_shapes=[pltpu.VMEM((tm, d), MXU_DTYPE), pltpu.VMEM((2, tm, tn // 2), jnp.float32)],
        compiler_params=_cparams(("parallel", "arbitrary")),
        name="mixer_in_proj",
    )(*args)


def _sortable_key(x):
    b = pltpu.bitcast(x, jnp.int32)
    return jnp.where(b < 0, b ^ jnp.int32(0x7FFFFFFF), b)


I16_MIN_ABS = 2 ** 15


def _indexer_kernel(qi_ref, kid_ref, wit_ref, bias_ref, key_ref, hi_ref, lo_ref, qm_ref, *, tq, tk, n_sel):
    i = pl.program_id(0)
    s = bias_ref.shape[0]
    n_live = (i * tq + tq + tk - 1) // tk
    krow = lax.broadcasted_iota(jnp.int32, (tk, tq), 0)
    qcol = i * tq + lax.broadcasted_iota(jnp.int32, (tk, tq), 1)
    lane = lax.broadcasted_iota(jnp.int32, (tq, LANES), 1)

    for p in range(IDX_HEADS // 2):
        qp = qi_ref[:, p * LANES:(p + 1) * LANES]
        zero = jnp.zeros_like(qp)
        qm_ref[2 * p] = jnp.where(lane < IDX_DIM, qp, zero)
        qm_ref[2 * p + 1] = jnp.where(lane >= IDX_DIM, qp, zero)

    def score_tile(c, carry):
        k0 = pl.multiple_of(c * tk, tk)
        kc = kid_ref[pl.ds(k0, tk), :]
        acc = jnp.zeros((tk, tq), jnp.float32)
        for h in range(IDX_HEADS):
            d = lax.dot_general(kc, qm_ref[h], _NT, preferred_element_type=jnp.float32)
            acc = acc + jnp.maximum(d, 0.0) * wit_ref[IDX_DIM + h:IDX_DIM + h + 1, :]
        causal = (k0 + krow) <= qcol
        key = _sortable_key(jnp.where(causal, acc, -jnp.inf))
        key_ref[pl.ds(k0, tk), :] = key
        hi_ref[pl.ds(k0, tk), :] = lax.shift_right_arithmetic(key, 16).astype(jnp.int16)
        lo_ref[pl.ds(k0, tk), :] = ((key & 0xFFFF) - I16_MIN_ABS).astype(jnp.int16)
        return carry

    lax.fori_loop(0, n_live, score_tile, 0)

    one16 = jnp.ones((tk, tq), jnp.int16)
    zero16 = jnp.zeros((tk, tq), jnp.int16)
    rows = 2 * BF16_SUBLANES

    def count_ge(ref, cand):
        cand16 = cand.astype(jnp.int16)

        def body(c, cnt):
            k0 = pl.multiple_of(c * tk, tk)
            ind = jnp.where(ref[pl.ds(k0, tk), :] >= cand16, one16, zero16)
            part = [cnt, jnp.zeros_like(cnt)]
            for r in range(tk // rows):
                part[r % 2] = part[r % 2] + ind[r * rows:(r + 1) * rows, :]
            return part[0] + part[1]
        cnt = lax.fori_loop(0, n_live, body, jnp.zeros((rows, tq), jnp.int16))
        return jnp.sum(cnt.astype(jnp.float32), axis=0, keepdims=True)

    def kth_largest16(ref, need):
        zero_c = jnp.zeros((1, tq), jnp.int32)
        tau = jnp.where(count_ge(ref, zero_c) >= need, zero_c, jnp.full((1, tq), -I16_MIN_ABS, jnp.int32))

        def bit_step(b, tau):
            cand = tau + lax.shift_left(jnp.int32(1), 14 - b)
            return jnp.where(count_ge(ref, cand) >= need, cand, tau)

        return lax.fori_loop(0, 15, bit_step, tau)

    need = jnp.full((1, tq), float(n_sel), jnp.float32)
    t_hi = kth_largest16(hi_ref, need)
    above = jnp.where(t_hi == I16_MIN_ABS - 1, 0.0,
                      count_ge(hi_ref, jnp.minimum(t_hi + 1, I16_MIN_ABS - 1)))
    t_hi16 = t_hi.astype(jnp.int16)

    def keep_ties(c, carry):
        k0 = pl.multiple_of(c * tk, tk)
        tie = hi_ref[pl.ds(k0, tk), :] == t_hi16
        lo_ref[pl.ds(k0, tk), :] = jnp.where(tie, lo_ref[pl.ds(k0, tk), :], jnp.full((tk, tq), -I16_MIN_ABS, jnp.int16))
        return carry

    lax.fori_loop(0, n_live, keep_ties, 0)
    t_lo = kth_largest16(lo_ref, need - above)
    tau = lax.shift_left(t_hi, 16) | ((t_lo + I16_MIN_ABS) & 0xFFFF)

    def write_live(c, carry):
        k0 = pl.multiple_of(c * tk, tk)
        causal = (k0 + krow) <= qcol
        sel = jnp.where(key_ref[pl.ds(k0, tk), :] >= tau, 0.0, NEG)
        bias_ref[pl.ds(k0, tk), :] = jnp.where(causal, sel, NEG).astype(bias_ref.dtype)
        return carry

    lax.fori_loop(0, n_live, write_live, 0)

    def write_dead(c, carry):
        k0 = pl.multiple_of(c * tk, tk)
        bias_ref[pl.ds(k0, tk), :] = jnp.full((tk, tq), NEG, bias_ref.dtype)
        return carry

    lax.fori_loop(n_live, s // tk, write_dead, 0)


def dsa_index_bias(qi, kid, wit, n_sel, tq=256, tk=512):
    s = qi.shape[0]
    return pl.pallas_call(
        functools.partial(_indexer_kernel, tq=tq, tk=tk, n_sel=n_sel),
        grid=(s // tq,),
        in_specs=[
            pl.BlockSpec((tq, IDX_HEADS * IDX_DIM), lambda i: (i, 0)),
            pl.BlockSpec((s, LANES), lambda i: (0, 0)),
            pl.BlockSpec((LANES, tq), lambda i: (0, i)),
        ],
        out_specs=pl.BlockSpec((None, s, tq), lambda i: (i, 0, 0)),
        out_shape=jax.ShapeDtypeStruct((s // tq, s, tq), MXU_DTYPE),
        scratch_shapes=[pltpu.VMEM((s, tq), jnp.int32),
                        pltpu.VMEM((s, tq), jnp.int16),
                        pltpu.VMEM((s, tq), jnp.int16),
                        pltpu.VMEM((IDX_HEADS, tq, LANES), MXU_DTYPE)],
        compiler_params=_cparams(("parallel",)),
        name="dsa_index_bias",
    )(qi, kid, wit)


def _flash_stage_a(h, s_t, m_ref, al_ref, s_ref):
    m_prev = m_ref[h]
    m_new = jnp.maximum(m_prev, jnp.max(s_t, axis=0, keepdims=True))
    al_ref[h] = jnp.exp2(m_prev - m_new)
    m_ref[h] = m_new
    s_ref[h % 2] = s_t


def _flash_stage_b(h, v_aug, m_ref, al_ref, s_ref, acc_ref):
    p_t = jnp.exp2(s_ref[h % 2] - m_ref[h]).astype(MXU_DTYPE)
    acc_ref[h] = al_ref[h] * acc_ref[h] + jnp.dot(v_aug, p_t, preferred_element_type=jnp.float32)


def _flash_tile(scores, values, m_ref, al_ref, s_ref, acc_ref):
    _flash_stage_a(0, scores(0), m_ref, al_ref, s_ref)
    for h in range(N_HEADS):
        if h + 1 < N_HEADS:
            _flash_stage_a(h + 1, scores(h + 1), m_ref, al_ref, s_ref)
        _flash_stage_b(h, values(h), m_ref, al_ref, s_ref, acc_ref)


def _flash_init(m_ref, acc_ref):
    m_ref[...] = jnp.full(m_ref.shape, NEG, jnp.float32)
    acc_ref[...] = jnp.zeros_like(acc_ref)


def _flash_finish(o_ref, acc_ref):
    for h in range(N_HEADS):
        out_t = acc_ref[h, 0:HEAD_DIM, :] / acc_ref[h, HEAD_DIM:HEAD_DIM + 1, :]
        o_ref[:, h * HEAD_DIM:(h + 1) * HEAD_DIM] = out_t.T.astype(o_ref.dtype)


def _flash_scratch(tq, tk):
    return [pltpu.VMEM((N_HEADS, 1, tq), jnp.float32),
            pltpu.VMEM((N_HEADS, 1, tq), jnp.float32),
            pltpu.VMEM((2, tk, tq), jnp.float32),
            pltpu.VMEM((N_HEADS, V_ROWS, tq), jnp.float32)]


def _dsa_attn_kernel(q_ref, k_ref, vt_ref, bias_ref, o_ref, m_ref, al_ref, s_ref, acc_ref, bias_f32_ref,
                     *, tq, tk):
    i, j = pl.program_id(0), pl.program_id(1)
    rep = N_HEADS // A_KV_HEADS

    @pl.when(j == 0)
    def _():
        _flash_init(m_ref, acc_ref)

    @pl.when(j * tk < (i + 1) * tq)
    def _():
        bias_f32_ref[...] = bias_ref[...].astype(jnp.float32)

        def scores(h):
            g = h // rep
            kg = k_ref[:, g * HEAD_DIM:(g + 1) * HEAD_DIM]
            qh = q_ref[:, h * HEAD_DIM:(h + 1) * HEAD_DIM]
            return lax.dot_general(kg, qh, _NT, preferred_element_type=jnp.float32) + bias_f32_ref[...]

        def values(h):
            g = h // rep
            return vt_ref[g * V_ROWS:(g + 1) * V_ROWS, :]

        _flash_tile(scores, values, m_ref, al_ref, s_ref, acc_ref)

    @pl.when(j == pl.num_programs(1) - 1)
    def _():
        _flash_finish(o_ref, acc_ref)


def dsa_attention(q, k, v_t, bias, tq=256, tk=1024):
    s = q.shape[0]
    tk = min(tk, s)
    kvw = A_KV_HEADS * HEAD_DIM

    def live_j(i, j):
        return jnp.minimum(j, ((i + 1) * tq - 1) // tk)

    return pl.pallas_call(
        functools.partial(_dsa_attn_kernel, tq=tq, tk=tk),
        grid=(s // tq, s // tk),
        in_specs=[
            pl.BlockSpec((tq, N_HEADS * HEAD_DIM), lambda i, j: (i, 0)),
            pl.BlockSpec((tk, kvw), lambda i, j: (live_j(i, j), 0)),
            pl.BlockSpec((A_KV_HEADS * V_ROWS, tk), lambda i, j: (0, live_j(i, j))),
            pl.BlockSpec((None, tk, tq), lambda i, j: (i, live_j(i, j), 0)),
        ],
        out_specs=pl.BlockSpec((tq, N_HEADS * HEAD_DIM), lambda i, j: (i, 0)),
        out_shape=jax.ShapeDtypeStruct((s, N_HEADS * HEAD_DIM), MXU_DTYPE),
        scratch_shapes=_flash_scratch(tq, tk) + [pltpu.VMEM((tk, tq), jnp.float32)],
        compiler_params=_cparams(("parallel", "arbitrary")),
        name="dsa_attention",
    )(q, k, v_t, bias)


def _moba_attn_kernel(q_ref, k_ref, vt_ref, km_ref, o_ref, m_ref, al_ref, s_ref, acc_ref, qa_ref, *, n_sel, bps):
    i, j = pl.program_id(0), pl.program_id(1)
    t = MOBA_BLOCK
    tk = bps * t

    @pl.when(j == 0)
    def _():
        _flash_init(m_ref, acc_ref)
        blk = lax.broadcasted_iota(jnp.int32, (LANES, t), 0)
        blk_f = blk.astype(jnp.float32)
        for h in range(N_HEADS):
            sl = slice(h * HEAD_DIM, (h + 1) * HEAD_DIM)
            gate = lax.dot_general(km_ref[:, sl].astype(MXU_DTYPE), q_ref[:, sl], _NT,
                                   preferred_element_type=jnp.float32)
            gate = jnp.where(blk < i, gate, -jnp.inf)
            bias = jnp.full((LANES, t), NEG, jnp.float32)
            for _ in range(n_sel):
                top = jnp.max(gate, axis=0, keepdims=True)
                first = jnp.min(jnp.where(gate == top, blk_f, float(LANES)), axis=0, keepdims=True)
                hit = blk_f == first
                bias = jnp.where(hit, 0.0, bias)
                gate = jnp.where(hit, -jnp.inf, gate)
            bias = jnp.where(blk < i, bias, NEG)
            qa_ref[h, :, 0:HEAD_DIM] = q_ref[:, sl]
            qa_ref[h, :, HEAD_DIM:HEAD_DIM + LANES] = bias.T.astype(qa_ref.dtype)

    def step(on_diagonal):
        shift = MOBA_BLOCK.bit_length() - 1
        key_blk = j * bps + lax.shift_right_logical(lax.broadcasted_iota(jnp.int32, (tk, LANES), 0), shift)
        slot = lax.broadcasted_iota(jnp.int32, (tk, LANES), 1)
        onehot = key_blk == slot
        if on_diagonal:
            onehot = jnp.logical_and(onehot, key_blk != i)
            kpos = j * tk + lax.broadcasted_iota(jnp.int32, (tk, t), 0)
            qpos = i * t + lax.broadcasted_iota(jnp.int32, (tk, t), 1)
            own_future = jnp.logical_and(lax.shift_right_logical(kpos, shift) == i, kpos > qpos)
            causal = jnp.where(own_future, NEG, 0.0)
        e = jnp.where(onehot, 1.0, 0.0).astype(MXU_DTYPE)

        def scores(h):
            k_aug = jnp.concatenate([k_ref[:, h * HEAD_DIM:(h + 1) * HEAD_DIM], e], axis=1)
            s_t = lax.dot_general(k_aug, qa_ref[h], _NT, preferred_element_type=jnp.float32)
            return s_t + causal if on_diagonal else s_t

        def values(h):
            return vt_ref[h * V_ROWS:(h + 1) * V_ROWS, :]

        _flash_tile(scores, values, m_ref, al_ref, s_ref, acc_ref)

    @pl.when((j + 1) * bps <= i)
    def _():
        step(False)

    @pl.when(jnp.logical_and(j * bps <= i, (j + 1) * bps > i))
    def _():
        step(True)

    @pl.when(j == pl.num_programs(1) - 1)
    def _():
        _flash_finish(o_ref, acc_ref)


def moba_attention(q, k, v_t, kmean_pad, n_sel, bps=4):
    s, hd = q.shape
    t = MOBA_BLOCK
    nb = s // t
    bps = min(bps, nb)
    tk = bps * t

    def live_j(i, j):
        return jnp.minimum(j, i // bps)

    return pl.pallas_call(
        functools.partial(_moba_attn_kernel, n_sel=n_sel, bps=bps),
        grid=(nb, nb // bps),
        in_specs=[
            pl.BlockSpec((t, hd), lambda i, j: (i, 0)),
            pl.BlockSpec((tk, hd), lambda i, j: (live_j(i, j), 0)),
            pl.BlockSpec((N_HEADS * V_ROWS, tk), lambda i, j: (0, live_j(i, j))),
            pl.BlockSpec((LANES, hd), lambda i, j: (0, 0)),
        ],
        out_specs=pl.BlockSpec((t, hd), lambda i, j: (i, 0)),
        out_shape=jax.ShapeDtypeStruct((s, hd), MXU_DTYPE),
        scratch_shapes=_flash_scratch(t, tk) + [pltpu.VMEM((N_HEADS, t, HEAD_DIM + LANES), MXU_DTYPE)],
        compiler_params=_cparams(("parallel", "arbitrary")),
        name="moba_attention",
    )(q, k, v_t, kmean_pad)


def _dsa_mixer(x, g, sc, sh, w_in, layer, tabs_main, tabs_idx):
    s = x.shape[0]
    hd = N_HEADS * HEAD_DIM
    kvw = A_KV_HEADS * HEAD_DIM
    regions = [("q", hd), ("k", kvw), ("v", kvw), ("qidx", IDX_HEADS * IDX_DIM)]
    n_main = sum(n for _, n in regions)
    n_tail = w_in.shape[2] - n_main
    w_tail = jnp.pad(w_in[layer, :, n_main:], ((0, 0), (0, LANES - n_tail)))
    q, k, v_t, qi, kid, wit = mixer_in_proj(x, g, sc, sh, w_in, layer, regions, tabs_main, tabs_idx, w_tail)
    n_sel = min(IDX_TOPK_MAX, s // 4)
    bias = dsa_index_bias(qi, kid, wit, n_sel)
    return dsa_attention(q, k, v_t, bias)


def _moba_mixer(x, g, sc, sh, w_in, layer, tabs_main):
    s = x.shape[0]
    hd = N_HEADS * HEAD_DIM
    tm = 1024
    q, k, kmean, v_t = mixer_in_proj(x, g, sc, sh, w_in, layer, [("q", hd), ("kmean", hd), ("v", hd)],
                                     tabs_main, tm=tm)
    nb = s // MOBA_BLOCK
    kmean = kmean[:, :tm // MOBA_BLOCK, :].reshape(nb, hd)
    kmean_pad = jnp.pad(kmean, ((0, LANES - nb), (0, 0)))
    return moba_attention(q, k, v_t, kmean_pad, min(MOBA_TOPK, nb))


def kernel(x, c, positions, a_w_in, a_w_o, b_w_in, b_w_o, ada_w, ada_b, norm_g, ffn_w_in, ffn_w_out):
    b, s, d = x.shape
    assert b == 1 and s % MOBA_BLOCK == 0 and s // MOBA_BLOCK <= LANES
    depth = ada_w.shape[0]
    xs = x[0]
    mod = ada_modulation(c, ada_w, ada_b)
    pos = positions[0]
    tabs_main = rope_tables(pos, ROT_DIM, HEAD_DIM)
    tabs_idx = rope_tables(pos, IDX_ROT_DIM, IDX_DIM)
    for i in range(depth):
        sh1, sc1, g1, sh2, sc2, g2 = [mod[i, :, t * d:(t + 1) * d] for t in range(6)]
        ng = [norm_g[i, t][None, :] for t in range(4)]
        if i % 2 == 0:
            o = _dsa_mixer(xs, ng[0], sc1, sh1, a_w_in, i // 2, tabs_main, tabs_idx)
            w_o = a_w_o
        else:
            o = _moba_mixer(xs, ng[0], sc1, sh1, b_w_in, i // 2, tabs_main)
            w_o = b_w_o
        xs = matmul_postnorm_residual(o, w_o, i // 2, xs, ng[1], g1)
        xs = mlp_sublayer(xs, ng[2], sc2, sh2, ffn_w_in, ffn_w_out, i, ng[3], g2)
    return xs[None]
```

```python
import functools
import math

import jax
import jax.numpy as jnp
from jax import lax
from jax.experimental import pallas as pl
from jax.experimental.pallas import tpu as pltpu

N_HEADS = 16
HEAD_DIM = 128
ROT_DIM = HEAD_DIM // 4
ROPE_THETA = 500000.0
EPS = 1e-6
A_KV_HEADS = 4
IDX_HEADS = 16
IDX_DIM = 64
IDX_ROT_DIM = IDX_DIM // 4
IDX_TOPK_MAX = 256
MOBA_BLOCK = 256
MOBA_TOPK = 3

LANES = 128
SUBLANES = 8
VMEM_LIMIT_BYTES = 56 * 1024 * 1024

MXU_DTYPE = jnp.bfloat16
BF16_SUBLANES = 16
V_ROWS = HEAD_DIM + BF16_SUBLANES
NEG = -1e30
LOG2E = math.log2(math.e)

_NT = (((1,), (1,)), ((), ()))


def _cparams(sem):
    return pltpu.CompilerParams(dimension_semantics=sem, vmem_limit_bytes=VMEM_LIMIT_BYTES)


def _ada_kernel(ct_ref, w_ref, b_ref, o_ref):
    ct = ct_ref[...]
    ca = ct * (1.0 / (1.0 + jnp.exp(-ct)))
    o_ref[...] = jnp.sum(ca * w_ref[...], axis=0, keepdims=True) + b_ref[...]


def ada_modulation(c, ada_w, ada_b, tn=1024):
    depth, d, n = ada_w.shape
    ct = c.reshape(d, 1)
    return pl.pallas_call(
        _ada_kernel,
        grid=(depth, n // tn),
        in_specs=[
            pl.BlockSpec((d, 1), lambda l, j: (0, 0)),
            pl.BlockSpec((None, d, tn), lambda l, j: (l, 0, j)),
            pl.BlockSpec((None, 1, tn), lambda l, j: (l, 0, j)),
        ],
        out_specs=pl.BlockSpec((None, 1, tn), lambda l, j: (l, 0, j)),
        out_shape=jax.ShapeDtypeStruct((depth, 1, n), jnp.float32),
        compiler_params=_cparams(("parallel", "parallel")),
        name="ada_modulation",
    )(ct, ada_w, ada_b.reshape(depth, 1, n))


def _norm_mod(x, g, sc, sh):
    ms = jnp.mean(x * x, axis=-1, keepdims=True)
    y = x * lax.rsqrt(ms + EPS)
    return (y * g) * (1.0 + sc) + sh


def _mlp_kernel(x_ref, g_ref, sc_ref, sh_ref, wu_ref, wg_ref, wo_ref, gp_ref, gate_ref, o_ref, h_ref):
    j = pl.program_id(1)

    @pl.when(j == 0)
    def _():
        h_ref[...] = _norm_mod(x_ref[...], g_ref[...], sc_ref[...], sh_ref[...]).astype(h_ref.dtype)
        o_ref[...] = jnp.zeros_like(o_ref)

    h = h_ref[...]
    u = jnp.dot(h, wu_ref[...].astype(MXU_DTYPE), preferred_element_type=jnp.float32)
    gt = jnp.dot(h, wg_ref[...].astype(MXU_DTYPE), preferred_element_type=jnp.float32)
    act = ((gt * (1.0 / (1.0 + jnp.exp(-gt)))) * u).astype(MXU_DTYPE)
    o_ref[...] += jnp.dot(act, wo_ref[...].astype(MXU_DTYPE), preferred_element_type=jnp.float32)

    @pl.when(j == pl.num_programs(1) - 1)
    def _():
        y = o_ref[...]
        ms = jnp.mean(y * y, axis=-1, keepdims=True)
        yn = (y * lax.rsqrt(ms + EPS)) * gp_ref[...]
        o_ref[...] = x_ref[...] + gate_ref[...] * yn


def mlp_sublayer(x, g, sc, sh, w_in, w_out, layer, g_post, gate, tm=1024, tf=256):
    s, d = x.shape
    f = w_in.shape[2] // 2
    nj = f // tf
    row = pl.BlockSpec((1, d), lambda i, j: (0, 0))
    return pl.pallas_call(
        _mlp_kernel,
        grid=(s // tm, nj),
        in_specs=[
            pl.BlockSpec((tm, d), lambda i, j: (i, 0), pipeline_mode=pl.Buffered(1)),
            row, row, row,
            pl.BlockSpec((None, d, tf), lambda i, j: (layer, 0, j)),
            pl.BlockSpec((None, d, tf), lambda i, j: (layer, 0, j + nj)),
            pl.BlockSpec((None, tf, d), lambda i, j: (layer, j, 0)),
            row, row,
        ],
        out_specs=pl.BlockSpec((tm, d), lambda i, j: (i, 0)),
        out_shape=jax.ShapeDtypeStruct((s, d), jnp.float32),
        scratch_shapes=[pltpu.VMEM((tm, d), MXU_DTYPE)],
        compiler_params=_cparams(("parallel", "arbitrary")),
        name="mlp_sublayer",
    )(x, g, sc, sh, w_in, w_in, w_out, g_post, gate)


def _mm_postnorm_kernel(a_ref, w_ref, x_ref, g_ref, gate_ref, o_ref):
    k = pl.program_id(1)

    @pl.when(k == 0)
    def _():
        o_ref[...] = jnp.zeros_like(o_ref)

    o_ref[...] += jnp.dot(a_ref[...], w_ref[...].astype(MXU_DTYPE), preferred_element_type=jnp.float32)

    @pl.when(k == pl.num_programs(1) - 1)
    def _():
        y = o_ref[...]
        ms = jnp.mean(y * y, axis=-1, keepdims=True)
        yn = (y * lax.rsqrt(ms + EPS)) * g_ref[...]
        o_ref[...] = x_ref[...] + gate_ref[...] * yn


def matmul_postnorm_residual(a, w, layer, x, g, gate, tm=1024, tk=512):
    s, kdim = a.shape
    d = w.shape[2]
    return pl.pallas_call(
        _mm_postnorm_kernel,
        grid=(s // tm, kdim // tk),
        in_specs=[
            pl.BlockSpec((tm, tk), lambda i, k: (i, k)),
            pl.BlockSpec((None, tk, d), lambda i, k: (layer, k, 0)),
            pl.BlockSpec((tm, d), lambda i, k: (i, 0), pipeline_mode=pl.Buffered(1)),
            pl.BlockSpec((1, d), lambda i, k: (0, 0)),
            pl.BlockSpec((1, d), lambda i, k: (0, 0)),
        ],
        out_specs=pl.BlockSpec((tm, d), lambda i, k: (i, 0)),
        out_shape=jax.ShapeDtypeStruct((s, d), jnp.float32),
        compiler_params=_cparams(("parallel", "arbitrary")),
        name="matmul_postnorm_residual",
    )(a, w, x, g, gate)


def rope_tables(positions, rot_dim, period):
    half = rot_dim // 2
    inv_freq = ROPE_THETA ** (-jnp.arange(half, dtype=jnp.float32) / half)
    ang = positions.astype(jnp.float32)[:, None] * inv_freq
    cos, sin = jnp.cos(ang), jnp.sin(ang)
    s = positions.shape[0]
    one = jnp.ones((s, period - rot_dim), jnp.float32)
    zero = jnp.zeros((s, period - rot_dim), jnp.float32)
    zh = jnp.zeros((s, half), jnp.float32)
    reps = LANES // period
    c = jnp.tile(jnp.concatenate([cos, cos, one], axis=1), (1, reps))
    sa = jnp.tile(jnp.concatenate([-sin, zh, zero], axis=1), (1, reps))
    sb = jnp.tile(jnp.concatenate([zh, sin, zero], axis=1), (1, reps))
    return c, sa, sb


def _rope_lanes(x, c, sa, sb, half):
    return x * c + pltpu.roll(x, LANES - half, 1) * sa + pltpu.roll(x, half, 1) * sb


def _in_proj_kernel(*refs, regions, n_main, has_tail, tm, tn):
    it = iter(refs)
    x_ref, g_ref, sc_ref, sh_ref, w_ref = (next(it) for _ in range(5))
    wt_ref = next(it) if has_tail else None
    tab_main = [next(it) for _ in range(3)]
    tab_idx = [next(it) for _ in range(3)] if has_tail else None
    outs = []
    for kind, _, _ in regions:
        outs.append([next(it), next(it)] if kind == "kmean" else [next(it)])
    tail_outs = [next(it), next(it)] if has_tail else None
    h_ref, acc_ref = next(it), next(it)

    j = pl.program_id(1)

    @pl.when(j == 0)
    def _():
        h_ref[...] = _norm_mod(x_ref[...], g_ref[...], sc_ref[...], sh_ref[...]).astype(h_ref.dtype)

    def rope(y, tabs, half):
        return _rope_lanes(y, tabs[0][...], tabs[1][...], tabs[2][...], half)

    def epilogue(kind, out, gi, y):
        sl = slice(gi * LANES, (gi + 1) * LANES)
        if kind == "q":
            out[0][:, sl] = (rope(y, tab_main, ROT_DIM // 2) * (HEAD_DIM ** -0.5 * LOG2E)).astype(out[0].dtype)
        elif kind == "k":
            out[0][:, sl] = rope(y, tab_main, ROT_DIM // 2).astype(out[0].dtype)
        elif kind == "kmean":
            r = rope(y, tab_main, ROT_DIM // 2)
            out[0][:, sl] = r.astype(out[0].dtype)
            for b in range(tm // MOBA_BLOCK):
                out[1][b:b + 1, sl] = jnp.mean(r[b * MOBA_BLOCK:(b + 1) * MOBA_BLOCK, :], axis=0, keepdims=True)
            out[1][tm // MOBA_BLOCK:, sl] = jnp.zeros((out[1].shape[0] - tm // MOBA_BLOCK, LANES), jnp.float32)
        elif kind == "v":
            r0 = gi * V_ROWS
            out[0][r0:r0 + HEAD_DIM, :] = y.T.astype(out[0].dtype)
            out[0][r0 + HEAD_DIM:r0 + V_ROWS, :] = jnp.ones((V_ROWS - HEAD_DIM, tm), out[0].dtype)
        elif kind == "qidx":
            out[0][:, sl] = rope(y, tab_idx, IDX_ROT_DIM // 2).astype(out[0].dtype)

    def region_step(kind, out):
        half_cols = tn // 2
        for hf in range(2):
            acc_ref[hf] = jnp.dot(h_ref[...], w_ref[:, hf * half_cols:(hf + 1) * half_cols].astype(MXU_DTYPE),
                                  preferred_element_type=jnp.float32)
        for hf in range(2):
            for g2 in range(half_cols // LANES):
                epilogue(kind, out, hf * (half_cols // LANES) + g2, acc_ref[hf, :, g2 * LANES:(g2 + 1) * LANES])

    for (kind, start, count), out in zip(regions, outs):
        @pl.when(jnp.logical_and(j >= start, j < start + count))
        def _(kind=kind, out=out):
            region_step(kind, out)

    if has_tail:
        @pl.when(j == n_main)
        def _():
            y = jnp.dot(h_ref[...], wt_ref[...].astype(MXU_DTYPE), preferred_element_type=jnp.float32)
            r = _rope_lanes(y, tab_idx[0][...], tab_idx[1][...], tab_idx[2][...], IDX_ROT_DIM // 2)
            lane = lax.broadcasted_iota(jnp.int32, y.shape, 1)
            kid_ref, wit_ref = tail_outs
            kid_ref[...] = jnp.where(lane < IDX_DIM, r, pltpu.roll(r, IDX_DIM, 1)).astype(kid_ref.dtype)
            wit_ref[...] = (y * (IDX_HEADS ** -0.5 * IDX_DIM ** -0.5)).T


def mixer_in_proj(x, g, sc, sh, w, layer, regions, tabs_main, tabs_idx=None, w_tail=None, tm=1024, tn=512):
    s, d = x.shape
    has_tail = w_tail is not None
    tiles, start = [], 0
    for kind, n_cols in regions:
        assert n_cols % tn == 0
        tiles.append((kind, start, n_cols // tn))
        start += n_cols // tn
    n_main = start
    heads_per_tile = tn // HEAD_DIM

    def clip(j, first, count):
        return jnp.clip(j - first, 0, count - 1)

    row = pl.BlockSpec((1, d), lambda i, j: (0, 0))
    tab = pl.BlockSpec((tm, LANES), lambda i, j: (i, 0))
    in_specs = [pl.BlockSpec((tm, d), lambda i, j: (i, 0)), row, row, row,
                pl.BlockSpec((None, d, tn), lambda i, j: (layer, 0, jnp.minimum(j, n_main - 1)))]
    args = [x, g, sc, sh, w]
    if has_tail:
        in_specs.append(pl.BlockSpec((d, LANES), lambda i, j: (0, 0)))
        args.append(w_tail)
    in_specs += [tab] * 3
    args += list(tabs_main)
    if has_tail:
        in_specs += [tab] * 3
        args += list(tabs_idx)

    out_specs, out_shape = [], []
    for kind, first, count in tiles:
        if kind == "v":
            out_specs.append(pl.BlockSpec((heads_per_tile * V_ROWS, tm),
                                          lambda i, j, first=first, count=count: (clip(j, first, count), i)))
            out_shape.append(jax.ShapeDtypeStruct((count * heads_per_tile * V_ROWS, s), MXU_DTYPE))
            continue
        out_specs.append(pl.BlockSpec((tm, tn), lambda i, j, first=first, count=count: (i, clip(j, first, count))))
        out_shape.append(jax.ShapeDtypeStruct((s, count * tn), MXU_DTYPE))
        if kind == "kmean":
            out_specs.append(pl.BlockSpec((None, SUBLANES, tn),
                                          lambda i, j, first=first, count=count: (i, 0, clip(j, first, count))))
            out_shape.append(jax.ShapeDtypeStruct((s // tm, SUBLANES, count * tn), jnp.float32))
    if has_tail:
        out_specs += [pl.BlockSpec((tm, LANES), lambda i, j: (i, 0)), pl.BlockSpec((LANES, tm), lambda i, j: (0, i))]
        out_shape += [jax.ShapeDtypeStruct((s, LANES), MXU_DTYPE), jax.ShapeDtypeStruct((LANES, s), jnp.float32)]

    return pl.pallas_call(
        functools.partial(_in_proj_kernel, regions=tiles, n_main=n_main, has_tail=has_tail, tm=tm, tn=tn),
        grid=(s // tm, n_main + (1 if has_tail else 0)),
        in_specs=in_specs,
        out_specs=out_specs,
        out_shape=out_shape,
        scratch_shapes=[pltpu.VMEM((tm, d), MXU_DTYPE), pltpu.VMEM((2, tm, tn // 2), jnp.float32)],
        compiler_params=_cparams(("parallel", "arbitrary")),
        name="mixer_in_proj",
    )(*args)


def _sortable_key(x):
    b = pltpu.bitcast(x, jnp.int32)
    return jnp.where(b < 0, b ^ jnp.int32(0x7FFFFFFF), b)


I16_MIN_ABS = 2 ** 15


def _indexer_kernel(qi_ref, kid_ref, wit_ref, bias_ref, key_ref, hi_ref, lo_ref, qm_ref, *, tq, tk, n_sel):
    i = pl.program_id(0)
    s = bias_ref.shape[0]
    n_live = (i * tq + tq + tk - 1) // tk
    krow = lax.broadcasted_iota(jnp.int32, (tk, tq), 0)
    qcol = i * tq + lax.broadcasted_iota(jnp.int32, (tk, tq), 1)
    lane = lax.broadcasted_iota(jnp.int32, (tq, LANES), 1)

    for p in range(IDX_HEADS // 2):
        qp = qi_ref[:, p * LANES:(p + 1) * LANES]
        zero = jnp.zeros_like(qp)
        qm_ref[2 * p] = jnp.where(lane < IDX_DIM, qp, zero)
        qm_ref[2 * p + 1] = jnp.where(lane >= IDX_DIM, qp, zero)

    def score_tile(c, carry):
        k0 = pl.multiple_of(c * tk, tk)
        kc = kid_ref[pl.ds(k0, tk), :]
        acc = jnp.zeros((tk, tq), jnp.float32)
        for h in range(IDX_HEADS):
            d = lax.dot_general(kc, qm_ref[h], _NT, preferred_element_type=jnp.float32)
            acc = acc + jnp.maximum(d, 0.0) * wit_ref[IDX_DIM + h:IDX_DIM + h + 1, :]
        causal = (k0 + krow) <= qcol
        key = _sortable_key(jnp.where(causal, acc, -jnp.inf))
        key_ref[pl.ds(k0, tk), :] = key
        hi_ref[pl.ds(k0, tk), :] = lax.shift_right_arithmetic(key, 16).astype(jnp.int16)
        lo_ref[pl.ds(k0, tk), :] = ((key & 0xFFFF) - I16_MIN_ABS).astype(jnp.int16)
        return carry

    lax.fori_loop(0, n_live, score_tile, 0)

    one16 = jnp.ones((tk, tq), jnp.int16)
    zero16 = jnp.zeros((tk, tq), jnp.int16)
    rows = 2 * BF16_SUBLANES

    def count_ge(ref, cand):
        cand16 = cand.astype(jnp.int16)

        def body(c, cnt):
            k0 = pl.multiple_of(c * tk, tk)
            ind = jnp.where(ref[pl.ds(k0, tk), :] >= cand16, one16, zero16)
            part = [cnt, jnp.zeros_like(cnt)]
            for r in range(tk // rows):
                part[r % 2] = part[r % 2] + ind[r * rows:(r + 1) * rows, :]
            return part[0] + part[1]
        cnt = lax.fori_loop(0, n_live, body, jnp.zeros((rows, tq), jnp.int16))
        return jnp.sum(cnt.astype(jnp.float32), axis=0, keepdims=True)

    def kth_largest16(ref, need):
        zero_c = jnp.zeros((1, tq), jnp.int32)
        tau = jnp.where(count_ge(ref, zero_c) >= need, zero_c, jnp.full((1, tq), -I16_MIN_ABS, jnp.int32))

        def bit_step(b, tau):
            cand = tau + lax.shift_left(jnp.int32(1), 14 - b)
            return jnp.where(count_ge(ref, cand) >= need, cand, tau)

        return lax.fori_loop(0, 15, bit_step, tau)

    need = jnp.full((1, tq), float(n_sel), jnp.float32)
    t_hi = kth_largest16(hi_ref, need)
    above = jnp.where(t_hi == I16_MIN_ABS - 1, 0.0,
                      count_ge(hi_ref, jnp.minimum(t_hi + 1, I16_MIN_ABS - 1)))
    t_hi16 = t_hi.astype(jnp.int16)

    def keep_ties(c, carry):
        k0 = pl.multiple_of(c * tk, tk)
        tie = hi_ref[pl.ds(k0, tk), :] == t_hi16
        lo_ref[pl.ds(k0, tk), :] = jnp.where(tie, lo_ref[pl.ds(k0, tk), :], jnp.full((tk, tq), -I16_MIN_ABS, jnp.int16))
        return carry

    lax.fori_loop(0, n_live, keep_ties, 0)
    t_lo = kth_largest16(lo_ref, need - above)
    tau = lax.shift_left(t_hi, 16) | ((t_lo + I16_MIN_ABS) & 0xFFFF)

    def write_live(c, carry):
        k0 = pl.multiple_of(c * tk, tk)
        causal = (k0 + krow) <= qcol
        sel = jnp.where(key_ref[pl.ds(k0, tk), :] >= tau, 0.0, NEG)
        bias_ref[pl.ds(k0, tk), :] = jnp.where(causal, sel, NEG).astype(bias_ref.dtype)
        return carry

    lax.fori_loop(0, n_live, write_live, 0)

    def write_dead(c, carry):
        k0 = pl.multiple_of(c * tk, tk)
        bias_ref[pl.ds(k0, tk), :] = jnp.full((tk, tq), NEG, bias_ref.dtype)
        return carry

    lax.fori_loop(n_live, s // tk, write_dead, 0)


def dsa_index_bias(qi, kid, wit, n_sel, tq=256, tk=512):
    s = qi.shape[0]
    return pl.pallas_call(
        functools.partial(_indexer_kernel, tq=tq, tk=tk, n_sel=n_sel),
        grid=(s // tq,),
        in_specs=[
            pl.BlockSpec((tq, IDX_HEADS * IDX_DIM), lambda i: (i, 0)),
            pl.BlockSpec((s, LANES), lambda i: (0, 0)),
            pl.BlockSpec((LANES, tq), lambda i: (0, i)),
        ],
        out_specs=pl.BlockSpec((None, s, tq), lambda i: (i, 0, 0)),
        out_shape=jax.ShapeDtypeStruct((s // tq, s, tq), MXU_DTYPE),
        scratch_shapes=[pltpu.VMEM((s, tq), jnp.int32),
                        pltpu.VMEM((s, tq), jnp.int16),
                        pltpu.VMEM((s, tq), jnp.int16),
                        pltpu.VMEM((IDX_HEADS, tq, LANES), MXU_DTYPE)],
        compiler_params=_cparams(("parallel",)),
        name="dsa_index_bias",
    )(qi, kid, wit)


def _flash_stage_a(h, s_t, m_ref, al_ref, s_ref):
    m_prev = m_ref[h]
    m_new = jnp.maximum(m_prev, jnp.max(s_t, axis=0, keepdims=True))
    al_ref[h] = jnp.exp2(m_prev - m_new)
    m_ref[h] = m_new
    s_ref[h % 2] = s_t


def _flash_stage_b(h, v_aug, m_ref, al_ref, s_ref, acc_ref):
    p_t = jnp.exp2(s_ref[h % 2] - m_ref[h]).astype(MXU_DTYPE)
    acc_ref[h] = al_ref[h] * acc_ref[h] + jnp.dot(v_aug, p_t, preferred_element_type=jnp.float32)


def _flash_tile(scores, values, m_ref, al_ref, s_ref, acc_ref):
    _flash_stage_a(0, scores(0), m_ref, al_ref, s_ref)
    for h in range(N_HEADS):
        if h + 1 < N_HEADS:
            _flash_stage_a(h + 1, scores(h + 1), m_ref, al_ref, s_ref)
        _flash_stage_b(h, values(h), m_ref, al_ref, s_ref, acc_ref)


def _flash_init(m_ref, acc_ref):
    m_ref[...] = jnp.full(m_ref.shape, NEG, jnp.float32)
    acc_ref[...] = jnp.zeros_like(acc_ref)


def _flash_finish(o_ref, acc_ref):
    for h in range(N_HEADS):
        out_t = acc_ref[h, 0:HEAD_DIM, :] / acc_ref[h, HEAD_DIM:HEAD_DIM + 1, :]
        o_ref[:, h * HEAD_DIM:(h + 1) * HEAD_DIM] = out_t.T.astype(o_ref.dtype)


def _flash_scratch(tq, tk):
    return [pltpu.VMEM((N_HEADS, 1, tq), jnp.float32),
            pltpu.VMEM((N_HEADS, 1, tq), jnp.float32),
            pltpu.VMEM((2, tk, tq), jnp.float32),
            pltpu.VMEM((N_HEADS, V_ROWS, tq), jnp.float32)]


def _dsa_attn_kernel(q_ref, k_ref, vt_ref, bias_ref, o_ref, m_ref, al_ref, s_ref, acc_ref, bias_f32_ref,
                     *, tq, tk):
    i, j = pl.program_id(0), pl.program_id(1)
    rep = N_HEADS // A_KV_HEADS

    @pl.when(j == 0)
    def _():
        _flash_init(m_ref, acc_ref)

    @pl.when(j * tk < (i + 1) * tq)
    def _():
        for b in range(bias_ref.shape[0]):
            w = bias_ref.shape[2]
            bias_f32_ref[:, b * w:(b + 1) * w] = bias_ref[b].astype(jnp.float32)

        def scores(h):
            g = h // rep
            kg = k_ref[:, g * HEAD_DIM:(g + 1) * HEAD_DIM]
            qh = q_ref[:, h * HEAD_DIM:(h + 1) * HEAD_DIM]
            return lax.dot_general(kg, qh, _NT, preferred_element_type=jnp.float32) + bias_f32_ref[...]

        def values(h):
            g = h // rep
            return vt_ref[g * V_ROWS:(g + 1) * V_ROWS, :]

        _flash_tile(scores, values, m_ref, al_ref, s_ref, acc_ref)

    @pl.when(j == pl.num_programs(1) - 1)
    def _():
        _flash_finish(o_ref, acc_ref)


def dsa_attention(q, k, v_t, bias, tq=512, tk=1024):
    s = q.shape[0]
    tk = min(tk, s)
    kvw = A_KV_HEADS * HEAD_DIM
    slabs = tq // bias.shape[2]
    assert slabs * bias.shape[2] == tq

    def live_j(i, j):
        return jnp.minimum(j, ((i + 1) * tq - 1) // tk)

    return pl.pallas_call(
        functools.partial(_dsa_attn_kernel, tq=tq, tk=tk),
        grid=(s // tq, s // tk),
        in_specs=[
            pl.BlockSpec((tq, N_HEADS * HEAD_DIM), lambda i, j: (i, 0)),
            pl.BlockSpec((tk, kvw), lambda i, j: (live_j(i, j), 0)),
            pl.BlockSpec((A_KV_HEADS * V_ROWS, tk), lambda i, j: (0, live_j(i, j))),
            pl.BlockSpec((slabs, tk, bias.shape[2]), lambda i, j: (i, live_j(i, j), 0)),
        ],
        out_specs=pl.BlockSpec((tq, N_HEADS * HEAD_DIM), lambda i, j: (i, 0)),
        out_shape=jax.ShapeDtypeStruct((s, N_HEADS * HEAD_DIM), MXU_DTYPE),
        scratch_shapes=_flash_scratch(tq, tk) + [pltpu.VMEM((tk, tq), jnp.float32)],
        compiler_params=_cparams(("parallel", "arbitrary")),
        name="dsa_attention",
    )(q, k, v_t, bias)


def _moba_attn_kernel(q_ref, k_ref, vt_ref, km_ref, o_ref, m_ref, al_ref, s_ref, acc_ref, qa_ref,
                      *, n_sel, bps, n_slots):
    i, j = pl.program_id(0), pl.program_id(1)
    t = MOBA_BLOCK
    tk = bps * t

    @pl.when(j == 0)
    def _():
        _flash_init(m_ref, acc_ref)
        blk = lax.broadcasted_iota(jnp.int32, (n_slots, t), 0)
        blk_f = blk.astype(jnp.float32)
        unused_slots = jnp.zeros((LANES - n_slots, t), jnp.float32)
        for h in range(N_HEADS):
            sl = slice(h * HEAD_DIM, (h + 1) * HEAD_DIM)
            gate = lax.dot_general(km_ref[0:n_slots, sl].astype(MXU_DTYPE), q_ref[:, sl], _NT,
                                   preferred_element_type=jnp.float32)
            gate = jnp.where(blk < i, gate, -jnp.inf)
            bias = jnp.full((n_slots, t), NEG, jnp.float32)
            for _ in range(n_sel):
                top = jnp.max(gate, axis=0, keepdims=True)
                first = jnp.min(jnp.where(gate == top, blk_f, float(LANES)), axis=0, keepdims=True)
                hit = blk_f == first
                bias = jnp.where(hit, 0.0, bias)
                gate = jnp.where(hit, -jnp.inf, gate)
            bias = jnp.where(blk < i, bias, NEG)
            bias = jnp.concatenate([bias, unused_slots], axis=0) if n_slots < LANES else bias
            qa_ref[h, :, 0:HEAD_DIM] = q_ref[:, sl]
            qa_ref[h, :, HEAD_DIM:HEAD_DIM + LANES] = bias.T.astype(qa_ref.dtype)

    def step(on_diagonal):
        shift = MOBA_BLOCK.bit_length() - 1
        key_blk = j * bps + lax.shift_right_logical(lax.broadcasted_iota(jnp.int32, (tk, LANES), 0), shift)
        slot = lax.broadcasted_iota(jnp.int32, (tk, LANES), 1)
        onehot = key_blk == slot
        if on_diagonal:
            onehot = jnp.logical_and(onehot, key_blk != i)
            kpos = j * tk + lax.broadcasted_iota(jnp.int32, (tk, t), 0)
            qpos = i * t + lax.broadcasted_iota(jnp.int32, (tk, t), 1)
            own_future = jnp.logical_and(lax.shift_right_logical(kpos, shift) == i, kpos > qpos)
            causal = jnp.where(own_future, NEG, 0.0)
        e = jnp.where(onehot, 1.0, 0.0).astype(MXU_DTYPE)

        def scores(h):
            k_aug = jnp.concatenate([k_ref[:, h * HEAD_DIM:(h + 1) * HEAD_DIM], e], axis=1)
            s_t = lax.dot_general(k_aug, qa_ref[h], _NT, preferred_element_type=jnp.float32)
            return s_t + causal if on_diagonal else s_t

        def values(h):
            return vt_ref[h * V_ROWS:(h + 1) * V_ROWS, :]

        _flash_tile(scores, values, m_ref, al_ref, s_ref, acc_ref)

    @pl.when((j + 1) * bps <= i)
    def _():
        step(False)

    @pl.when(jnp.logical_and(j * bps <= i, (j + 1) * bps > i))
    def _():
        step(True)

    @pl.when(j == pl.num_programs(1) - 1)
    def _():
        _flash_finish(o_ref, acc_ref)


def moba_attention(q, k, v_t, kmean_pad, n_sel, bps=4):
    s, hd = q.shape
    t = MOBA_BLOCK
    nb = s // t
    bps = min(bps, nb)
    tk = bps * t

    def live_j(i, j):
        return jnp.minimum(j, i // bps)

    return pl.pallas_call(
        functools.partial(_moba_attn_kernel, n_sel=n_sel, bps=bps,
                          n_slots=min(LANES, -(-nb // BF16_SUBLANES) * BF16_SUBLANES)),
        grid=(nb, nb // bps),
        in_specs=[
            pl.BlockSpec((t, hd), lambda i, j: (i, 0)),
            pl.BlockSpec((tk, hd), lambda i, j: (live_j(i, j), 0)),
            pl.BlockSpec((N_HEADS * V_ROWS, tk), lambda i, j: (0, live_j(i, j))),
            pl.BlockSpec((LANES, hd), lambda i, j: (0, 0)),
        ],
        out_specs=pl.BlockSpec((t, hd), lambda i, j: (i, 0)),
        out_shape=jax.ShapeDtypeStruct((s, hd), MXU_DTYPE),
        scratch_shapes=_flash_scratch(t, tk) + [pltpu.VMEM((N_HEADS, t, HEAD_DIM + LANES), MXU_DTYPE)],
        compiler_params=_cparams(("parallel", "arbitrary")),
        name="moba_attention",
    )(q, k, v_t, kmean_pad)


def _dsa_mixer(x, g, sc, sh, w_in, layer, tabs_main, tabs_idx):
    s = x.shape[0]
    hd = N_HEADS * HEAD_DIM
    kvw = A_KV_HEADS * HEAD_DIM
    regions = [("q", hd), ("k", kvw), ("v", kvw), ("qidx", IDX_HEADS * IDX_DIM)]
    n_main = sum(n for _, n in regions)
    n_tail = w_in.shape[2] - n_main
    w_tail = jnp.pad(w_in[layer, :, n_main:], ((0, 0), (0, LANES - n_tail)))
    q, k, v_t, qi, kid, wit = mixer_in_proj(x, g, sc, sh, w_in, layer, regions, tabs_main, tabs_idx, w_tail)
    n_sel = min(IDX_TOPK_MAX, s // 4)
    bias = dsa_index_bias(qi, kid, wit, n_sel)
    return dsa_attention(q, k, v_t, bias)


def _moba_mixer(x, g, sc, sh, w_in, layer, tabs_main):
    s = x.shape[0]
    hd = N_HEADS * HEAD_DIM
    tm = 1024
    q, k, kmean, v_t = mixer_in_proj(x, g, sc, sh, w_in, layer, [("q", hd), ("kmean", hd), ("v", hd)],
                                     tabs_main, tm=tm)
    nb = s // MOBA_BLOCK
    kmean = kmean[:, :tm // MOBA_BLOCK, :].reshape(nb, hd)
    kmean_pad = jnp.pad(kmean, ((0, LANES - nb), (0, 0)))
    return moba_attention(q, k, v_t, kmean_pad, min(MOBA_TOPK, nb))


def kernel(x, c, positions, a_w_in, a_w_o, b_w_in, b_w_o, ada_w, ada_b, norm_g, ffn_w_in, ffn_w_out):
    b, s, d = x.shape
    assert b == 1 and s % MOBA_BLOCK == 0 and s // MOBA_BLOCK <= LANES
    depth = ada_w.shape[0]
    xs = x[0]
    mod = ada_modulation(c, ada_w, ada_b)
    pos = positions[0]
    tabs_main = rope_tables(pos, ROT_DIM, HEAD_DIM)
    tabs_idx = rope_tables(pos, IDX_ROT_DIM, IDX_DIM)
    for i in range(depth):
        sh1, sc1, g1, sh2, sc2, g2 = [mod[i, :, t * d:(t + 1) * d] for t in range(6)]
        ng = [norm_g[i, t][None, :] for t in range(4)]
        if i % 2 == 0:
            o = _dsa_mixer(xs, ng[0], sc1, sh1, a_w_in, i // 2, tabs_main, tabs_idx)
            w_o = a_w_o
        else:
            o = _moba_mixer(xs, ng[0], sc1, sh1, b_w_in, i // 2, tabs_main)
            w_o = b_w_o
        xs = matmul_postnorm_residual(o, w_o, i // 2, xs, ng[1], g1)
        xs = mlp_sublayer(xs, ng[2], sc2, sh2, ffn_w_in, ffn_w_out, i, ng[3], g2)
    return xs[None]
```

```python
import functools
import math

import jax
import jax.numpy as jnp
from jax import lax
from jax.experimental import pallas as pl
from jax.experimental.pallas import tpu as pltpu

N_HEADS = 16
HEAD_DIM = 128
ROT_DIM = HEAD_DIM // 4
ROPE_THETA = 500000.0
EPS = 1e-6
A_KV_HEADS = 4
IDX_HEADS = 16
IDX_DIM = 64
IDX_ROT_DIM = IDX_DIM // 4
IDX_TOPK_MAX = 256
MOBA_BLOCK = 256
MOBA_TOPK = 3

LANES = 128
SUBLANES = 8
VMEM_LIMIT_BYTES = 56 * 1024 * 1024

MXU_DTYPE = jnp.bfloat16
BF16_SUBLANES = 16
V_ROWS = HEAD_DIM + BF16_SUBLANES
NEG = -1e30
LOG2E = math.log2(math.e)

_NT = (((1,), (1,)), ((), ()))


def _cparams(sem):
    return pltpu.CompilerParams(dimension_semantics=sem, vmem_limit_bytes=VMEM_LIMIT_BYTES)


def _ada_kernel(ct_ref, w_ref, b_ref, o_ref):
    ct = ct_ref[...]
    ca = ct * (1.0 / (1.0 + jnp.exp(-ct)))
    o_ref[...] = jnp.sum(ca * w_ref[...], axis=0, keepdims=True) + b_ref[...]


def ada_modulation(c, ada_w, ada_b, tn=1024):
    depth, d, n = ada_w.shape
    ct = c.reshape(d, 1)
    return pl.pallas_call(
        _ada_kernel,
        grid=(depth, n // tn),
        in_specs=[
            pl.BlockSpec((d, 1), lambda l, j: (0, 0)),
            pl.BlockSpec((None, d, tn), lambda l, j: (l, 0, j)),
            pl.BlockSpec((None, 1, tn), lambda l, j: (l, 0, j)),
        ],
        out_specs=pl.BlockSpec((None, 1, tn), lambda l, j: (l, 0, j)),
        out_shape=jax.ShapeDtypeStruct((depth, 1, n), jnp.float32),
        compiler_params=_cparams(("parallel", "parallel")),
        name="ada_modulation",
    )(ct, ada_w, ada_b.reshape(depth, 1, n))


def _norm_mod(x, g, sc, sh):
    ms = jnp.mean(x * x, axis=-1, keepdims=True)
    y = x * lax.rsqrt(ms + EPS)
    return (y * g) * (1.0 + sc) + sh


def _mlp_kernel(x_ref, g_ref, sc_ref, sh_ref, wu_ref, wg_ref, wo_ref, gp_ref, gate_ref, o_ref, h_ref):
    j = pl.program_id(1)

    @pl.when(j == 0)
    def _():
        h_ref[...] = _norm_mod(x_ref[...], g_ref[...], sc_ref[...], sh_ref[...]).astype(h_ref.dtype)
        o_ref[...] = jnp.zeros_like(o_ref)

    h = h_ref[...]
    u = jnp.dot(h, wu_ref[...].astype(MXU_DTYPE), preferred_element_type=jnp.float32)
    gt = jnp.dot(h, wg_ref[...].astype(MXU_DTYPE), preferred_element_type=jnp.float32)
    act = ((gt * (1.0 / (1.0 + jnp.exp(-gt)))) * u).astype(MXU_DTYPE)
    o_ref[...] += jnp.dot(act, wo_ref[...].astype(MXU_DTYPE), preferred_element_type=jnp.float32)

    @pl.when(j == pl.num_programs(1) - 1)
    def _():
        y = o_ref[...]
        ms = jnp.mean(y * y, axis=-1, keepdims=True)
        yn = (y * lax.rsqrt(ms + EPS)) * gp_ref[...]
        o_ref[...] = x_ref[...] + gate_ref[...] * yn


def mlp_sublayer(x, g, sc, sh, w_in, w_out, layer, g_post, gate, tm=1024, tf=256):
    s, d = x.shape
    f = w_in.shape[2] // 2
    nj = f // tf
    row = pl.BlockSpec((1, d), lambda i, j: (0, 0))
    return pl.pallas_call(
        _mlp_kernel,
        grid=(s // tm, nj),
        in_specs=[
            pl.BlockSpec((tm, d), lambda i, j: (i, 0), pipeline_mode=pl.Buffered(1)),
            row, row, row,
            pl.BlockSpec((None, d, tf), lambda i, j: (layer, 0, j)),
            pl.BlockSpec((None, d, tf), lambda i, j: (layer, 0, j + nj)),
            pl.BlockSpec((None, tf, d), lambda i, j: (layer, j, 0)),
            row, row,
        ],
        out_specs=pl.BlockSpec((tm, d), lambda i, j: (i, 0)),
        out_shape=jax.ShapeDtypeStruct((s, d), jnp.float32),
        scratch_shapes=[pltpu.VMEM((tm, d), MXU_DTYPE)],
        compiler_params=_cparams(("parallel", "arbitrary")),
        name="mlp_sublayer",
    )(x, g, sc, sh, w_in, w_in, w_out, g_post, gate)


def _mm_postnorm_kernel(a_ref, w_ref, x_ref, g_ref, gate_ref, o_ref):
    k = pl.program_id(1)

    @pl.when(k == 0)
    def _():
        o_ref[...] = jnp.zeros_like(o_ref)

    o_ref[...] += jnp.dot(a_ref[...], w_ref[...].astype(MXU_DTYPE), preferred_element_type=jnp.float32)

    @pl.when(k == pl.num_programs(1) - 1)
    def _():
        y = o_ref[...]
        ms = jnp.mean(y * y, axis=-1, keepdims=True)
        yn = (y * lax.rsqrt(ms + EPS)) * g_ref[...]
        o_ref[...] = x_ref[...] + gate_ref[...] * yn


def matmul_postnorm_residual(a, w, layer, x, g, gate, tm=1024, tk=512):
    s, kdim = a.shape
    d = w.shape[2]
    return pl.pallas_call(
        _mm_postnorm_kernel,
        grid=(s // tm, kdim // tk),
        in_specs=[
            pl.BlockSpec((tm, tk), lambda i, k: (i, k)),
            pl.BlockSpec((None, tk, d), lambda i, k: (layer, k, 0)),
            pl.BlockSpec((tm, d), lambda i, k: (i, 0), pipeline_mode=pl.Buffered(1)),
            pl.BlockSpec((1, d), lambda i, k: (0, 0)),
            pl.BlockSpec((1, d), lambda i, k: (0, 0)),
        ],
        out_specs=pl.BlockSpec((tm, d), lambda i, k: (i, 0)),
        out_shape=jax.ShapeDtypeStruct((s, d), jnp.float32),
        compiler_params=_cparams(("parallel", "arbitrary")),
        name="matmul_postnorm_residual",
    )(a, w, x, g, gate)


def rope_tables(positions, rot_dim, period):
    half = rot_dim // 2
    inv_freq = ROPE_THETA ** (-jnp.arange(half, dtype=jnp.float32) / half)
    ang = positions.astype(jnp.float32)[:, None] * inv_freq
    cos, sin = jnp.cos(ang), jnp.sin(ang)
    s = positions.shape[0]
    one = jnp.ones((s, period - rot_dim), jnp.float32)
    zero = jnp.zeros((s, period - rot_dim), jnp.float32)
    zh = jnp.zeros((s, half), jnp.float32)
    reps = LANES // period
    c = jnp.tile(jnp.concatenate([cos, cos, one], axis=1), (1, reps))
    sa = jnp.tile(jnp.concatenate([-sin, zh, zero], axis=1), (1, reps))
    sb = jnp.tile(jnp.concatenate([zh, sin, zero], axis=1), (1, reps))
    return c, sa, sb


def _rope_lanes(x, c, sa, sb, half):
    return x * c + pltpu.roll(x, LANES - half, 1) * sa + pltpu.roll(x, half, 1) * sb


def _in_proj_kernel(*refs, regions, n_main, has_tail, tm, tn):
    it = iter(refs)
    x_ref, g_ref, sc_ref, sh_ref, w_ref = (next(it) for _ in range(5))
    wt_ref = next(it) if has_tail else None
    tab_main = [next(it) for _ in range(3)]
    tab_idx = [next(it) for _ in range(3)] if has_tail else None
    outs = []
    for kind, _, _ in regions:
        outs.append([next(it), next(it)] if kind == "kmean" else [next(it)])
    tail_outs = [next(it), next(it)] if has_tail else None
    h_ref, acc_ref = next(it), next(it)

    j = pl.program_id(1)

    @pl.when(j == 0)
    def _():
        h_ref[...] = _norm_mod(x_ref[...], g_ref[...], sc_ref[...], sh_ref[...]).astype(h_ref.dtype)

    def rope(y, tabs, half):
        return _rope_lanes(y, tabs[0][...], tabs[1][...], tabs[2][...], half)

    def epilogue(kind, out, gi, y):
        sl = slice(gi * LANES, (gi + 1) * LANES)
        if kind == "q":
            out[0][:, sl] = (rope(y, tab_main, ROT_DIM // 2) * (HEAD_DIM ** -0.5 * LOG2E)).astype(out[0].dtype)
        elif kind == "k":
            out[0][:, sl] = rope(y, tab_main, ROT_DIM // 2).astype(out[0].dtype)
        elif kind == "kmean":
            r = rope(y, tab_main, ROT_DIM // 2)
            out[0][:, sl] = r.astype(out[0].dtype)
            for b in range(tm // MOBA_BLOCK):
                out[1][b:b + 1, sl] = jnp.mean(r[b * MOBA_BLOCK:(b + 1) * MOBA_BLOCK, :], axis=0, keepdims=True)
            out[1][tm // MOBA_BLOCK:, sl] = jnp.zeros((out[1].shape[0] - tm // MOBA_BLOCK, LANES), jnp.float32)
        elif kind == "v":
            r0 = gi * V_ROWS
            out[0][r0:r0 + HEAD_DIM, :] = y.T.astype(out[0].dtype)
            out[0][r0 + HEAD_DIM:r0 + V_ROWS, :] = jnp.ones((V_ROWS - HEAD_DIM, tm), out[0].dtype)
        elif kind == "qidx":
            out[0][:, sl] = rope(y, tab_idx, IDX_ROT_DIM // 2).astype(out[0].dtype)

    def region_step(kind, out):
        half_cols = tn // 2
        for hf in range(2):
            acc_ref[hf] = jnp.dot(h_ref[...], w_ref[:, hf * half_cols:(hf + 1) * half_cols].astype(MXU_DTYPE),
                                  preferred_element_type=jnp.float32)
        for hf in range(2):
            for g2 in range(half_cols // LANES):
                epilogue(kind, out, hf * (half_cols // LANES) + g2, acc_ref[hf, :, g2 * LANES:(g2 + 1) * LANES])

    for (kind, start, count), out in zip(regions, outs):
        @pl.when(jnp.logical_and(j >= start, j < start + count))
        def _(kind=kind, out=out):
            region_step(kind, out)

    if has_tail:
        @pl.when(j == n_main)
        def _():
            y = jnp.dot(h_ref[...], wt_ref[...].astype(MXU_DTYPE), preferred_element_type=jnp.float32)
            r = _rope_lanes(y, tab_idx[0][...], tab_idx[1][...], tab_idx[2][...], IDX_ROT_DIM // 2)
            lane = lax.broadcasted_iota(jnp.int32, y.shape, 1)
            kid_ref, wit_ref = tail_outs
            kid_ref[...] = jnp.where(lane < IDX_DIM, r, pltpu.roll(r, IDX_DIM, 1)).astype(kid_ref.dtype)
            wit_ref[...] = (y * (IDX_HEADS ** -0.5 * IDX_DIM ** -0.5)).T


def mixer_in_proj(x, g, sc, sh, w, layer, regions, tabs_main, tabs_idx=None, w_tail=None, tm=1024, tn=512):
    s, d = x.shape
    has_tail = w_tail is not None
    tiles, start = [], 0
    for kind, n_cols in regions:
        assert n_cols % tn == 0
        tiles.append((kind, start, n_cols // tn))
        start += n_cols // tn
    n_main = start
    heads_per_tile = tn // HEAD_DIM

    def clip(j, first, count):
        return jnp.clip(j - first, 0, count - 1)

    row = pl.BlockSpec((1, d), lambda i, j: (0, 0))
    tab = pl.BlockSpec((tm, LANES), lambda i, j: (i, 0))
    in_specs = [pl.BlockSpec((tm, d), lambda i, j: (i, 0)), row, row, row,
                pl.BlockSpec((None, d, tn), lambda i, j: (layer, 0, jnp.minimum(j, n_main - 1)))]
    args = [x, g, sc, sh, w]
    if has_tail:
        in_specs.append(pl.BlockSpec((d, LANES), lambda i, j: (0, 0)))
        args.append(w_tail)
    in_specs += [tab] * 3
    args += list(tabs_main)
    if has_tail:
        in_specs += [tab] * 3
        args += list(tabs_idx)

    out_specs, out_shape = [], []
    for kind, first, count in tiles:
        if kind == "v":
            out_specs.append(pl.BlockSpec((heads_per_tile * V_ROWS, tm),
                                          lambda i, j, first=first, count=count: (clip(j, first, count), i)))
            out_shape.append(jax.ShapeDtypeStruct((count * heads_per_tile * V_ROWS, s), MXU_DTYPE))
            continue
        out_specs.append(pl.BlockSpec((tm, tn), lambda i, j, first=first, count=count: (i, clip(j, first, count))))
        out_shape.append(jax.ShapeDtypeStruct((s, count * tn), MXU_DTYPE))
        if kind == "kmean":
            out_specs.append(pl.BlockSpec((None, SUBLANES, tn),
                                          lambda i, j, first=first, count=count: (i, 0, clip(j, first, count))))
            out_shape.append(jax.ShapeDtypeStruct((s // tm, SUBLANES, count * tn), jnp.float32))
    if has_tail:
        out_specs += [pl.BlockSpec((tm, LANES), lambda i, j: (i, 0)), pl.BlockSpec((LANES, tm), lambda i, j: (0, i))]
        out_shape += [jax.ShapeDtypeStruct((s, LANES), MXU_DTYPE), jax.ShapeDtypeStruct((LANES, s), jnp.float32)]

    return pl.pallas_call(
        functools.partial(_in_proj_kernel, regions=tiles, n_main=n_main, has_tail=has_tail, tm=tm, tn=tn),
        grid=(s // tm, n_main + (1 if has_tail else 0)),
        in_specs=in_specs,
        out_specs=out_specs,
        out_shape=out_shape,
        scratch_shapes=[pltpu.VMEM((tm, d), MXU_DTYPE), pltpu.VMEM((2, tm, tn // 2), jnp.float32)],
        compiler_params=_cparams(("parallel", "arbitrary")),
        name="mixer_in_proj",
    )(*args)


def _sortable_key(x):
    b = pltpu.bitcast(x, jnp.int32)
    return jnp.where(b < 0, b ^ jnp.int32(0x7FFFFFFF), b)


I16_MIN_ABS = 2 ** 15


def _indexer_kernel(qi_ref, kid_ref, wit_ref, bias_ref, key_ref, hi_ref, lo_ref, qm_ref, *, tq, tk, n_sel):
    i = pl.program_id(0)
    s = bias_ref.shape[0]
    n_live = (i * tq + tq + tk - 1) // tk
    krow = lax.broadcasted_iota(jnp.int32, (tk, tq), 0)
    qcol = i * tq + lax.broadcasted_iota(jnp.int32, (tk, tq), 1)
    lane = lax.broadcasted_iota(jnp.int32, (tq, LANES), 1)

    for p in range(IDX_HEADS // 2):
        qp = qi_ref[:, p * LANES:(p + 1) * LANES]
        zero = jnp.zeros_like(qp)
        qm_ref[2 * p] = jnp.where(lane < IDX_DIM, qp, zero)
        qm_ref[2 * p + 1] = jnp.where(lane >= IDX_DIM, qp, zero)

    def score_tile(c, carry):
        k0 = pl.multiple_of(c * tk, tk)
        kc = kid_ref[pl.ds(k0, tk), :]
        acc = jnp.zeros((tk, tq), jnp.float32)
        for h in range(IDX_HEADS):
            d = lax.dot_general(kc, qm_ref[h], _NT, preferred_element_type=jnp.float32)
            acc = acc + jnp.maximum(d, 0.0) * wit_ref[IDX_DIM + h:IDX_DIM + h + 1, :]
        causal = (k0 + krow) <= qcol
        key = _sortable_key(jnp.where(causal, acc, -jnp.inf))
        key_ref[pl.ds(k0, tk), :] = key
        hi_ref[pl.ds(k0, tk), :] = lax.shift_right_arithmetic(key, 16).astype(jnp.int16)
        lo_ref[pl.ds(k0, tk), :] = ((key & 0xFFFF) - I16_MIN_ABS).astype(jnp.int16)
        return carry

    lax.fori_loop(0, n_live, score_tile, 0)

    one16 = jnp.ones((tk, tq), jnp.int16)
    zero16 = jnp.zeros((tk, tq), jnp.int16)
    rows = 2 * BF16_SUBLANES

    def count_ge(ref, cand):
        cand16 = cand.astype(jnp.int16)

        def body(c, cnt):
            k0 = pl.multiple_of(c * tk, tk)
            ind = jnp.where(ref[pl.ds(k0, tk), :] >= cand16, one16, zero16)
            part = [cnt, jnp.zeros_like(cnt)]
            for r in range(tk // rows):
                part[r % 2] = part[r % 2] + ind[r * rows:(r + 1) * rows, :]
            return part[0] + part[1]
        cnt = lax.fori_loop(0, n_live, body, jnp.zeros((rows, tq), jnp.int16))
        return jnp.sum(cnt.astype(jnp.float32), axis=0, keepdims=True)

    def kth_largest16(ref, need):
        zero_c = jnp.zeros((1, tq), jnp.int32)
        tau = jnp.where(count_ge(ref, zero_c) >= need, zero_c, jnp.full((1, tq), -I16_MIN_ABS, jnp.int32))

        def bit_step(b, tau):
            cand = tau + lax.shift_left(jnp.int32(1), 14 - b)
            return jnp.where(count_ge(ref, cand) >= need, cand, tau)

        return lax.fori_loop(0, 15, bit_step, tau)

    need = jnp.full((1, tq), float(n_sel), jnp.float32)
    t_hi = kth_largest16(hi_ref, need)
    above = jnp.where(t_hi == I16_MIN_ABS - 1, 0.0,
                      count_ge(hi_ref, jnp.minimum(t_hi + 1, I16_MIN_ABS - 1)))
    t_hi16 = t_hi.astype(jnp.int16)

    def keep_ties(c, carry):
        k0 = pl.multiple_of(c * tk, tk)
        tie = hi_ref[pl.ds(k0, tk), :] == t_hi16
        lo_ref[pl.ds(k0, tk), :] = jnp.where(tie, lo_ref[pl.ds(k0, tk), :], jnp.full((tk, tq), -I16_MIN_ABS, jnp.int16))
        return carry

    lax.fori_loop(0, n_live, keep_ties, 0)
    t_lo = kth_largest16(lo_ref, need - above)
    tau = lax.shift_left(t_hi, 16) | ((t_lo + I16_MIN_ABS) & 0xFFFF)

    def write_live(c, carry):
        k0 = pl.multiple_of(c * tk, tk)
        causal = (k0 + krow) <= qcol
        sel = jnp.where(key_ref[pl.ds(k0, tk), :] >= tau, 0.0, NEG)
        bias_ref[pl.ds(k0, tk), :] = jnp.where(causal, sel, NEG).astype(bias_ref.dtype)
        return carry

    lax.fori_loop(0, n_live, write_live, 0)

    def write_dead(c, carry):
        k0 = pl.multiple_of(c * tk, tk)
        bias_ref[pl.ds(k0, tk), :] = jnp.full((tk, tq), NEG, bias_ref.dtype)
        return carry

    lax.fori_loop(n_live, s // tk, write_dead, 0)


def dsa_index_bias(qi, kid, wit, n_sel, tq=256, tk=512):
    s = qi.shape[0]
    return pl.pallas_call(
        functools.partial(_indexer_kernel, tq=tq, tk=tk, n_sel=n_sel),
        grid=(s // tq,),
        in_specs=[
            pl.BlockSpec((tq, IDX_HEADS * IDX_DIM), lambda i: (i, 0)),
            pl.BlockSpec((s, LANES), lambda i: (0, 0)),
            pl.BlockSpec((LANES, tq), lambda i: (0, i)),
        ],
        out_specs=pl.BlockSpec((None, s, tq), lambda i: (i, 0, 0)),
        out_shape=jax.ShapeDtypeStruct((s // tq, s, tq), MXU_DTYPE),
        scratch_shapes=[pltpu.VMEM((s, tq), jnp.int32),
                        pltpu.VMEM((s, tq), jnp.int16),
                        pltpu.VMEM((s, tq), jnp.int16),
                        pltpu.VMEM((IDX_HEADS, tq, LANES), MXU_DTYPE)],
        compiler_params=_cparams(("parallel",)),
        name="dsa_index_bias",
    )(qi, kid, wit)


def _flash_stage_a(h, s_t, m_ref, al_ref, s_ref):
    m_prev = m_ref[h]
    m_new = jnp.maximum(m_prev, jnp.max(s_t, axis=0, keepdims=True))
    al_ref[h] = jnp.exp2(m_prev - m_new)
    m_ref[h] = m_new
    s_ref[h % 2] = s_t


def _flash_stage_b(h, v_aug, m_ref, al_ref, s_ref, acc_ref):
    p_t = jnp.exp2(s_ref[h % 2] - m_ref[h]).astype(MXU_DTYPE)
    acc_ref[h] = al_ref[h] * acc_ref[h] + jnp.dot(v_aug, p_t, preferred_element_type=jnp.float32)


def _flash_tile(scores, values, m_ref, al_ref, s_ref, acc_ref):
    _flash_stage_a(0, scores(0), m_ref, al_ref, s_ref)
    for h in range(N_HEADS):
        if h + 1 < N_HEADS:
            _flash_stage_a(h + 1, scores(h + 1), m_ref, al_ref, s_ref)
        _flash_stage_b(h, values(h), m_ref, al_ref, s_ref, acc_ref)


def _flash_init(m_ref, acc_ref):
    m_ref[...] = jnp.full(m_ref.shape, NEG, jnp.float32)
    acc_ref[...] = jnp.zeros_like(acc_ref)


def _flash_finish(o_ref, acc_ref):
    for h in range(N_HEADS):
        out_t = acc_ref[h, 0:HEAD_DIM, :] / acc_ref[h, HEAD_DIM:HEAD_DIM + 1, :]
        o_ref[:, h * HEAD_DIM:(h + 1) * HEAD_DIM] = out_t.T.astype(o_ref.dtype)


def _flash_scratch(tq, tk):
    return [pltpu.VMEM((N_HEADS, 1, tq), jnp.float32),
            pltpu.VMEM((N_HEADS, 1, tq), jnp.float32),
            pltpu.VMEM((2, tk, tq), jnp.float32),
            pltpu.VMEM((N_HEADS, V_ROWS, tq), jnp.float32)]


def _dsa_attn_kernel(q_ref, k_ref, vt_ref, bias_ref, o_ref, m_ref, al_ref, s_ref, acc_ref, bias_f32_ref,
                     *, tq, tk):
    i, j = pl.program_id(0), pl.program_id(1)
    rep = N_HEADS // A_KV_HEADS

    @pl.when(j == 0)
    def _():
        _flash_init(m_ref, acc_ref)

    @pl.when(j * tk < (i + 1) * tq)
    def _():
        for b in range(bias_ref.shape[0]):
            w = bias_ref.shape[2]
            bias_f32_ref[:, b * w:(b + 1) * w] = bias_ref[b].astype(jnp.float32)

        def scores(h):
            g = h // rep
            kg = k_ref[:, g * HEAD_DIM:(g + 1) * HEAD_DIM]
            qh = q_ref[:, h * HEAD_DIM:(h + 1) * HEAD_DIM]
            return lax.dot_general(kg, qh, _NT, preferred_element_type=jnp.float32) + bias_f32_ref[...]

        def values(h):
            g = h // rep
            return vt_ref[g * V_ROWS:(g + 1) * V_ROWS, :]

        _flash_tile(scores, values, m_ref, al_ref, s_ref, acc_ref)

    @pl.when(j == pl.num_programs(1) - 1)
    def _():
        _flash_finish(o_ref, acc_ref)


def dsa_attention(q, k, v_t, bias, tq=512, tk=1024):
    s = q.shape[0]
    tk = min(tk, s)
    kvw = A_KV_HEADS * HEAD_DIM
    slabs = tq // bias.shape[2]
    assert slabs * bias.shape[2] == tq

    def live_j(i, j):
        return jnp.minimum(j, ((i + 1) * tq - 1) // tk)

    return pl.pallas_call(
        functools.partial(_dsa_attn_kernel, tq=tq, tk=tk),
        grid=(s // tq, s // tk),
        in_specs=[
            pl.BlockSpec((tq, N_HEADS * HEAD_DIM), lambda i, j: (i, 0)),
            pl.BlockSpec((tk, kvw), lambda i, j: (live_j(i, j), 0)),
            pl.BlockSpec((A_KV_HEADS * V_ROWS, tk), lambda i, j: (0, live_j(i, j))),
            pl.BlockSpec((slabs, tk, bias.shape[2]), lambda i, j: (i, live_j(i, j), 0)),
        ],
        out_specs=pl.BlockSpec((tq, N_HEADS * HEAD_DIM), lambda i, j: (i, 0)),
        out_shape=jax.ShapeDtypeStruct((s, N_HEADS * HEAD_DIM), MXU_DTYPE),
        scratch_shapes=_flash_scratch(tq, tk) + [pltpu.VMEM((tk, tq), jnp.float32)],
        compiler_params=_cparams(("parallel", "arbitrary")),
        name="dsa_attention",
    )(q, k, v_t, bias)


def _moba_attn_kernel(q_ref, k_ref, vt_ref, km_ref, o_ref, m_ref, al_ref, s_ref, acc_ref, qa_ref,
                      *, n_sel, bps, n_slots, tq):
    i, j = pl.program_id(0), pl.program_id(1)
    t = MOBA_BLOCK
    tk = bps * t
    shift = MOBA_BLOCK.bit_length() - 1
    first_blk = i * (tq // t)
    last_blk = first_blk + tq // t - 1

    @pl.when(j == 0)
    def _():
        _flash_init(m_ref, acc_ref)
        blk = lax.broadcasted_iota(jnp.int32, (n_slots, tq), 0)
        blk_f = blk.astype(jnp.float32)
        own = first_blk + lax.shift_right_logical(lax.broadcasted_iota(jnp.int32, (n_slots, tq), 1), shift)
        unused_slots = jnp.zeros((LANES - n_slots, tq), jnp.float32)
        for h in range(N_HEADS):
            sl = slice(h * HEAD_DIM, (h + 1) * HEAD_DIM)
            gate = lax.dot_general(km_ref[0:n_slots, sl].astype(MXU_DTYPE), q_ref[:, sl], _NT,
                                   preferred_element_type=jnp.float32)
            gate = jnp.where(blk < own, gate, -jnp.inf)
            bias = jnp.full((n_slots, tq), NEG, jnp.float32)
            for _ in range(n_sel):
                top = jnp.max(gate, axis=0, keepdims=True)
                first = jnp.min(jnp.where(gate == top, blk_f, float(LANES)), axis=0, keepdims=True)
                hit = blk_f == first
                bias = jnp.where(hit, 0.0, bias)
                gate = jnp.where(hit, -jnp.inf, gate)
            bias = jnp.where(blk < own, bias, jnp.where(blk == own, 0.0, NEG))
            bias = jnp.concatenate([bias, unused_slots], axis=0) if n_slots < LANES else bias
            qa_ref[h, :, 0:HEAD_DIM] = q_ref[:, sl]
            qa_ref[h, :, HEAD_DIM:HEAD_DIM + LANES] = bias.T.astype(qa_ref.dtype)

    def step(on_diagonal):
        key_blk = j * bps + lax.shift_right_logical(lax.broadcasted_iota(jnp.int32, (tk, LANES), 0), shift)
        slot = lax.broadcasted_iota(jnp.int32, (tk, LANES), 1)
        e = jnp.where(key_blk == slot, 1.0, 0.0).astype(MXU_DTYPE)
        if on_diagonal:
            kpos = j * tk + lax.broadcasted_iota(jnp.int32, (tk, tq), 0)
            qpos = i * tq + lax.broadcasted_iota(jnp.int32, (tk, tq), 1)
            same_blk = lax.shift_right_logical(kpos, shift) == lax.shift_right_logical(qpos, shift)
            causal = jnp.where(jnp.logical_and(same_blk, kpos > qpos), NEG, 0.0)

        def scores(h):
            k_aug = jnp.concatenate([k_ref[:, h * HEAD_DIM:(h + 1) * HEAD_DIM], e], axis=1)
            s_t = lax.dot_general(k_aug, qa_ref[h], _NT, preferred_element_type=jnp.float32)
            return s_t + causal if on_diagonal else s_t

        def values(h):
            return vt_ref[h * V_ROWS:(h + 1) * V_ROWS, :]

        _flash_tile(scores, values, m_ref, al_ref, s_ref, acc_ref)

    @pl.when((j + 1) * bps <= first_blk)
    def _():
        step(False)

    @pl.when(jnp.logical_and(j * bps <= last_blk, (j + 1) * bps > first_blk))
    def _():
        step(True)

    @pl.when(j == pl.num_programs(1) - 1)
    def _():
        _flash_finish(o_ref, acc_ref)


def moba_attention(q, k, v_t, kmean_pad, n_sel, bps=4, tq=512):
    s, hd = q.shape
    t = MOBA_BLOCK
    nb = s // t
    bps = min(bps, nb)
    tk = bps * t
    qb = tq // t

    def live_j(i, j):
        return jnp.minimum(j, (i * qb + qb - 1) // bps)

    return pl.pallas_call(
        functools.partial(_moba_attn_kernel, n_sel=n_sel, bps=bps, tq=tq,
                          n_slots=min(LANES, -(-nb // BF16_SUBLANES) * BF16_SUBLANES)),
        grid=(s // tq, nb // bps),
        in_specs=[
            pl.BlockSpec((tq, hd), lambda i, j: (i, 0)),
            pl.BlockSpec((tk, hd), lambda i, j: (live_j(i, j), 0)),
            pl.BlockSpec((N_HEADS * V_ROWS, tk), lambda i, j: (0, live_j(i, j))),
            pl.BlockSpec((LANES, hd), lambda i, j: (0, 0)),
        ],
        out_specs=pl.BlockSpec((tq, hd), lambda i, j: (i, 0)),
        out_shape=jax.ShapeDtypeStruct((s, hd), MXU_DTYPE),
        scratch_shapes=_flash_scratch(tq, tk) + [pltpu.VMEM((N_HEADS, tq, HEAD_DIM + LANES), MXU_DTYPE)],
        compiler_params=_cparams(("parallel", "arbitrary")),
        name="moba_attention",
    )(q, k, v_t, kmean_pad)


def _dsa_mixer(x, g, sc, sh, w_in, layer, tabs_main, tabs_idx):
    s = x.shape[0]
    hd = N_HEADS * HEAD_DIM
    kvw = A_KV_HEADS * HEAD_DIM
    regions = [("q", hd), ("k", kvw), ("v", kvw), ("qidx", IDX_HEADS * IDX_DIM)]
    n_main = sum(n for _, n in regions)
    n_tail = w_in.shape[2] - n_main
    w_tail = jnp.pad(w_in[layer, :, n_main:], ((0, 0), (0, LANES - n_tail)))
    q, k, v_t, qi, kid, wit = mixer_in_proj(x, g, sc, sh, w_in, layer, regions, tabs_main, tabs_idx, w_tail)
    n_sel = min(IDX_TOPK_MAX, s // 4)
    bias = dsa_index_bias(qi, kid, wit, n_sel)
    return dsa_attention(q, k, v_t, bias)


def _moba_mixer(x, g, sc, sh, w_in, layer, tabs_main):
    s = x.shape[0]
    hd = N_HEADS * HEAD_DIM
    tm = 1024
    q, k, kmean, v_t = mixer_in_proj(x, g, sc, sh, w_in, layer, [("q", hd), ("kmean", hd), ("v", hd)],
                                     tabs_main, tm=tm)
    nb = s // MOBA_BLOCK
    kmean = kmean[:, :tm // MOBA_BLOCK, :].reshape(nb, hd)
    kmean_pad = jnp.pad(kmean, ((0, LANES - nb), (0, 0)))
    return moba_attention(q, k, v_t, kmean_pad, min(MOBA_TOPK, nb))


def kernel(x, c, positions, a_w_in, a_w_o, b_w_in, b_w_o, ada_w, ada_b, norm_g, ffn_w_in, ffn_w_out):
    b, s, d = x.shape
    assert b == 1 and s % MOBA_BLOCK == 0 and s // MOBA_BLOCK <= LANES
    depth = ada_w.shape[0]
    xs = x[0]
    mod = ada_modulation(c, ada_w, ada_b)
    pos = positions[0]
    tabs_main = rope_tables(pos, ROT_DIM, HEAD_DIM)
    tabs_idx = rope_tables(pos, IDX_ROT_DIM, IDX_DIM)
    for i in range(depth):
        sh1, sc1, g1, sh2, sc2, g2 = [mod[i, :, t * d:(t + 1) * d] for t in range(6)]
        ng = [norm_g[i, t][None, :] for t in range(4)]
        if i % 2 == 0:
            o = _dsa_mixer(xs, ng[0], sc1, sh1, a_w_in, i // 2, tabs_main, tabs_idx)
            w_o = a_w_o
        else:
            o = _moba_mixer(xs, ng[0], sc1, sh1, b_w_in, i // 2, tabs_main)
            w_o = b_w_o
        xs = matmul_postnorm_residual(o, w_o, i // 2, xs, ng[1], g1)
        xs = mlp_sublayer(xs, ng[2], sc2, sh2, ffn_w_in, ffn_w_out, i, ng[3], g2)
    return xs[None]
```
